```python
import math
import jax, jax.numpy as jnp
from jax import lax
import numpy as np

D_MODEL = 2048
BATCH = 4
SEQ = 2048
DEPTH = 4

CHUNK = 64
Q_BLOCK = 128
HEAD_DIM = 128
D_MIX = D_MODEL
D_SB = D_MIX // 2
D_CH = D_MIX - D_SB
N_HEADS_SB = D_SB // HEAD_DIM
N_HEADS_CH = D_CH // HEAD_DIM
LEFT_CHUNKS = 8
BAND = LEFT_CHUNKS + 1
REL_CLIP = 256
N_REL = REL_CLIP + CHUNK
D_IN = 4 * D_SB + 4 * D_CH
NORM_EPS = 1e-6
NEG_BIG = -1e30

kernel_name = "hybrid_stickbreak_chunkband_trunk"


def rms_norm(x, g):
    xf = x.astype(jnp.float32)
    y = xf * lax.rsqrt(jnp.mean(xf * xf, axis=-1, keepdims=True) + NORM_EPS)
    return (y * g.astype(jnp.float32)).astype(x.dtype)


def split_heads(t, n_heads):
    b, s, _ = t.shape
    return t.reshape(b, s, n_heads, HEAD_DIM).transpose(0, 2, 1, 3)


def merge_heads(t):
    b, h, s, d = t.shape
    return t.transpose(0, 2, 1, 3).reshape(b, s, h * d)


def stick_breaking_attention(q, k, v):
    seq = q.shape[2]
    scale = q.shape[-1] ** -0.5
    outs = []
    for blk in range(seq // Q_BLOCK):
        t0 = blk * Q_BLOCK
        t1 = t0 + Q_BLOCK
        qb = q[:, :, t0:t1]
        kb = k[:, :, :t1]
        vb = v[:, :, :t1]
        z = jnp.einsum('bhtd,bhsd->bhts', qb, kb).astype(jnp.float32) * scale
        t_idx = jnp.arange(t0, t1)[:, None]
        s_idx = jnp.arange(t1)[None, :]
        causal = s_idx < t_idx
        log_stay = jnp.where(causal, jax.nn.log_sigmoid(-z), 0.0)
        after = lax.cumsum(log_stay, axis=3, reverse=True) - log_stay
        w = jnp.where(causal, jnp.exp(jax.nn.log_sigmoid(z) + after), 0.0)
        outs.append(jnp.einsum('bhts,bhsd->bhtd', w.astype(v.dtype), vb))
    return jnp.concatenate(outs, axis=2)


def chunk_band_attention(q, k, v, q_gain, k_gain, rel_table):
    b, h, seq, d = q.shape
    nc = seq // CHUNK
    q = rms_norm(q, q_gain)
    k = rms_norm(k, k_gain)
    qc = q.reshape(b, h, nc, CHUNK, d)
    pad = ((0, 0), (0, 0), (LEFT_CHUNKS * CHUNK, 0), (0, 0))
    kc = jnp.pad(k, pad).reshape(b, h, nc + LEFT_CHUNKS, CHUNK, d)
    vc = jnp.pad(v, pad).reshape(b, h, nc + LEFT_CHUNKS, CHUNK, d)
    band_idx = jnp.arange(nc)[:, None] + jnp.arange(BAND)[None, :]
    kband = kc[:, :, band_idx].reshape(b, h, nc, BAND * CHUNK, d)
    vband = vc[:, :, band_idx].reshape(b, h, nc, BAND * CHUNK, d)
    scores = jnp.einsum('bhcid,bhcpd->bhcip', qc, kband).astype(jnp.float32) * (d ** -0.5)
    i_pos = np.arange(CHUNK)[:, None]
    p_pos = np.arange(BAND * CHUNK)[None, :]
    dist = LEFT_CHUNKS * CHUNK + i_pos - p_pos
    rel_idx = np.clip(dist, -(CHUNK - 1), REL_CLIP) + (CHUNK - 1)
    bias = rel_table.astype(jnp.float32)[:, rel_idx]
    scores = scores + bias[None, :, None]
    valid = jnp.repeat(band_idx >= LEFT_CHUNKS, CHUNK, axis=1)
    scores = jnp.where(valid[None, None, :, None, :], scores, NEG_BIG)
    probs = jax.nn.softmax(scores, axis=-1)
    out = jnp.einsum('bhcip,bhcpd->bhcid', probs.astype(v.dtype), vband)
    return out.reshape(b, h, seq, d)


def setup_inputs(seed: int = 0) -> dict:
    key = jax.random.key(seed)
    ks = jax.random.split(key, 8)
    x = jax.random.normal(ks[0], (BATCH, SEQ, D_MODEL), jnp.float32)
    norm_g = 1.0 + 0.02 * jax.random.normal(ks[1], (DEPTH, D_MODEL), jnp.float32)
    w_in = jax.random.normal(ks[2], (DEPTH, D_MODEL, D_IN), jnp.float32) * D_MODEL ** -0.5
    q_norm_g = 1.0 + 0.02 * jax.random.normal(ks[3], (DEPTH, HEAD_DIM), jnp.float32)
    k_norm_g = 1.0 + 0.02 * jax.random.normal(ks[4], (DEPTH, HEAD_DIM), jnp.float32)
    rel_bias = 0.1 * jax.random.normal(ks[5], (DEPTH, N_HEADS_CH, N_REL), jnp.float32)
    w_out = jax.random.normal(ks[6], (DEPTH, D_MIX, D_MODEL), jnp.float32) * D_MIX ** -0.5
    return {"x": x, "norm_g": norm_g, "w_in": w_in, "q_norm_g": q_norm_g,
            "k_norm_g": k_norm_g, "rel_bias": rel_bias, "w_out": w_out}


def reference(x, norm_g, w_in, q_norm_g, k_norm_g, rel_bias, w_out):
    splits = np.cumsum([D_SB, D_SB, D_SB, D_SB, D_CH, D_CH, D_CH])
    for layer in range(DEPTH):
        h = rms_norm(x, norm_g[layer])
        proj = jnp.einsum('bsd,de->bse', h, w_in[layer])
        qa, ka, va, ga, qb, kb, vb, gb = jnp.split(proj, splits, axis=-1)
        ya = stick_breaking_attention(split_heads(qa, N_HEADS_SB),
                                      split_heads(ka, N_HEADS_SB),
                                      split_heads(va, N_HEADS_SB))
        yb = chunk_band_attention(split_heads(qb, N_HEADS_CH),
                                  split_heads(kb, N_HEADS_CH),
                                  split_heads(vb, N_HEADS_CH),
                                  q_norm_g[layer], k_norm_g[layer], rel_bias[layer])
        mixed = jnp.concatenate([merge_heads(ya) * jax.nn.silu(ga),
                                 merge_heads(yb) * jax.nn.silu(gb)], axis=-1)
        x = x + jnp.einsum('bse,ed->bsd', mixed, w_out[layer])
    return x
```

```python
import functools

import jax
import jax.numpy as jnp
import numpy as np
from jax import lax
from jax.experimental import pallas as pl
from jax.experimental.pallas import tpu as pltpu

HEAD_DIM = 128
CHUNK = 64
LEFT_CHUNKS = 8
REL_CLIP = 256
NORM_EPS = 1e-6
NEG_BIG = -1e30

V7X_LANES = 128
V7X_VMEM_BYTES = 64 * 1024 * 1024

F32 = jnp.float32
BF16 = jnp.bfloat16


def _vmem_limit(block_bytes):
    return int(min(2 * block_bytes + 24 * 1024 * 1024, V7X_VMEM_BYTES - 8 * 1024 * 1024))


def _in_proj_kernel(x_ref, g_ref, w_ref, o_ref, h_ref):
    @pl.when(pl.program_id(1) == 0)
    def _():
        x = x_ref[...]
        ms = jnp.mean(x * x, axis=-1, keepdims=True)
        h_ref[...] = (x * lax.rsqrt(ms + NORM_EPS) * g_ref[...]).astype(BF16)

    o_ref[...] = jnp.dot(h_ref[...], w_ref[...], preferred_element_type=F32).astype(o_ref.dtype)


def _in_proj(x2, g, w_bf, *, tm, tn):
    m, d = x2.shape
    n = w_bf.shape[1]
    blk = tm * d * 4 + d * tn * 2 + tm * tn * 2 + tm * d * 2
    return pl.pallas_call(
        _in_proj_kernel,
        grid=(m // tm, n // tn),
        in_specs=[
            pl.BlockSpec((tm, d), lambda i, j: (i, 0)),
            pl.BlockSpec((1, d), lambda i, j: (0, 0)),
            pl.BlockSpec((d, tn), lambda i, j: (0, j)),
        ],
        out_specs=pl.BlockSpec((tm, tn), lambda i, j: (i, j)),
        out_shape=jax.ShapeDtypeStruct((m, n), BF16),
        scratch_shapes=[pltpu.VMEM((tm, d), BF16)],
        compiler_params=pltpu.CompilerParams(
            dimension_semantics=("parallel", "arbitrary"), vmem_limit_bytes=_vmem_limit(blk)),
        name="in_proj",
    )(x2, g, w_bf)


def _out_proj_kernel(x_ref, ma_ref, mb_ref, w_ref, o_ref):
    da = ma_ref.shape[1]
    acc = jnp.dot(ma_ref[...], w_ref[:da, :], preferred_element_type=F32)
    acc = acc + jnp.dot(mb_ref[...], w_ref[da:, :], preferred_element_type=F32)
    o_ref[...] = x_ref[...] + acc


def _out_proj(x2, ma, mb, w_bf, *, tm):
    m, d = x2.shape
    da, db = ma.shape[1], mb.shape[1]
    blk = tm * d * 4 * 2 + tm * (da + db) * 2 + (da + db) * d * 2
    return pl.pallas_call(
        _out_proj_kernel,
        grid=(m // tm,),
        in_specs=[
            pl.BlockSpec((tm, d), lambda i: (i, 0)),
            pl.BlockSpec((tm, da), lambda i: (i, 0)),
            pl.BlockSpec((tm, db), lambda i: (i, 0)),
            pl.BlockSpec((da + db, d), lambda i: (0, 0)),
        ],
        out_specs=pl.BlockSpec((tm, d), lambda i: (i, 0)),
        out_shape=jax.ShapeDtypeStruct((m, d), F32),
        compiler_params=pltpu.CompilerParams(
            dimension_semantics=("parallel",), vmem_limit_bytes=_vmem_limit(blk)),
        name="out_proj",
    )(x2, ma, mb, w_bf)


def _silu(g):
    return g * (1.0 / (1.0 + jnp.exp(-g)))


def _sb_attn_kernel(q_ref, k_ref, v_ref, g_ref, o_ref, acc_ref, car_ref, *, nh, tq, tk, scale):
    seq = q_ref.shape[0]
    d = HEAD_DIM
    nq = seq // tq
    kpq = tq // tk

    row = lax.broadcasted_iota(jnp.int32, (2 * tk, 2 * tk), 0) % tk
    col = lax.broadcasted_iota(jnp.int32, (2 * tk, 2 * tk), 1)
    uj = jnp.where((col >= tk) | (row > col), 1.0, 0.0).astype(BF16)
    t_minus_s = (lax.broadcasted_iota(jnp.int32, (tq, tk), 0)
                 - lax.broadcasted_iota(jnp.int32, (tq, tk), 1))

    def pair(h, q, k0, t0, masked):
        hs = pl.ds(h * d, d)
        k = k_ref[pl.ds(k0, tk), hs]
        v = v_ref[pl.ds(k0, tk), hs]
        z = lax.dot_general(q, k, (((1,), (1,)), ((), ())), preferred_element_type=F32) * scale
        sp = jnp.maximum(z, 0.0) + jnp.log1p(jnp.exp(-jnp.abs(z)))
        if masked:
            causal = t_minus_s > (k0 - t0)
            spm = jnp.where(causal, sp, 0.0)
        else:
            spm = sp
        hi = spm.astype(BF16)
        lo = (spm - hi.astype(F32)).astype(BF16)
        cs = jnp.dot(jnp.concatenate([hi, lo], axis=1), uj, preferred_element_type=F32)
        car = car_ref[h]
        w = jnp.exp(z - sp - cs[:, :tk] - car)
        if masked:
            w = jnp.where(causal, w, 0.0)
        car_ref[h] = car + cs[:, tk:]
        acc_ref[h] += jnp.dot(w.astype(BF16), v, preferred_element_type=F32)

    def q_block(i, _):
        t0 = pl.multiple_of(i * tq, tq)
        acc_ref[...] = jnp.zeros_like(acc_ref)
        car_ref[...] = jnp.zeros_like(car_ref)
        qs = [q_ref[pl.ds(t0, tq), pl.ds(h * d, d)] for h in range(nh)]
        for jj in range(kpq):
            k0 = pl.multiple_of(t0 + (kpq - 1 - jj) * tk, tk)
            for h in range(nh):
                pair(h, qs[h], k0, t0, True)

        def k_block(jj, _):
            k0 = pl.multiple_of(t0 - (jj + 1) * tk, tk)
            for h in range(nh):
                pair(h, qs[h], k0, t0, False)
            return 0

        lax.fori_loop(0, i * kpq, k_block, 0)
        for h in range(nh):
            g = g_ref[pl.ds(t0, tq), pl.ds(h * d, d)].astype(F32)
            o_ref[pl.ds(t0, tq), pl.ds(h * d, d)] = (acc_ref[h] * _silu(g)).astype(o_ref.dtype)
        return 0

    lax.fori_loop(0, nq, q_block, 0)


def _sb_attn(proj, *, batch, seq, n_heads, col_q, col_k, col_v, col_g, nh, tq, tk):
    wblk = nh * HEAD_DIM
    assert n_heads % nh == 0 and seq % tq == 0 and tq % tk == 0 and tk == V7X_LANES

    def spec(col):
        base = col // wblk
        return pl.BlockSpec((seq, wblk), lambda b, hg: (b, base + hg))

    blk = 5 * seq * wblk * 2
    kern = functools.partial(_sb_attn_kernel, nh=nh, tq=tq, tk=tk, scale=HEAD_DIM ** -0.5)
    return pl.pallas_call(
        kern,
        grid=(batch, n_heads // nh),
        in_specs=[spec(col_q), spec(col_k), spec(col_v), spec(col_g)],
        out_specs=pl.BlockSpec((seq, wblk), lambda b, hg: (b, hg)),
        out_shape=jax.ShapeDtypeStruct((batch * seq, n_heads * HEAD_DIM), BF16),
        scratch_shapes=[pltpu.VMEM((nh, tq, HEAD_DIM), F32), pltpu.VMEM((nh, tq, HEAD_DIM), F32)],
        compiler_params=pltpu.CompilerParams(
            dimension_semantics=("parallel", "parallel"), vmem_limit_bytes=_vmem_limit(blk)),
        name="sb_attn",
    )(proj, proj, proj, proj)


_CH_TQ = 2 * CHUNK
_CH_WIN = (LEFT_CHUNKS + 2) * CHUNK
_CH_PAD = LEFT_CHUNKS * CHUNK
_CH_BASE = _CH_WIN + _CH_TQ


def _rel_base_index():
    u = np.arange(_CH_BASE)
    u = np.where(u >= _CH_WIN, u - _CH_BASE, u)
    return (np.clip(_CH_PAD - u, -(CHUNK - 1), REL_CLIP) + (CHUNK - 1)).astype(np.int32)


def _ch_attn_kernel(q_ref, k_ref, v_ref, g_ref, qg_ref, kg_ref, base_ref, o_ref,
                    kn_ref, vp_ref, bias_ref, *, scale):
    seq = q_ref.shape[0]
    tq, win, pad = _CH_TQ, _CH_WIN, _CH_PAD

    def rms(x, gain):
        ms = jnp.mean(x * x, axis=-1, keepdims=True)
        return x * lax.rsqrt(ms + NORM_EPS) * gain

    kn_ref[:pad, :] = jnp.zeros((pad, HEAD_DIM), BF16)
    vp_ref[:pad, :] = jnp.zeros((pad, HEAD_DIM), BF16)
    kn_ref[pad:, :] = rms(k_ref[...].astype(F32), kg_ref[...]).astype(BF16)
    vp_ref[pad:, :] = v_ref[...]
    base = jnp.broadcast_to(base_ref[...], (tq, _CH_BASE))
    bias_ref[...] = pltpu.roll(base, 0, 1, stride=1, stride_axis=0)[:, :win]

    qc = lax.broadcasted_iota(jnp.int32, (tq, win), 0) // CHUNK
    kc = lax.broadcasted_iota(jnp.int32, (tq, win), 1) // CHUNK
    band = (kc >= qc) & (kc <= qc + LEFT_CHUNKS)

    def q_block(i, _):
        t0 = pl.multiple_of(i * tq, tq)
        qn = rms(q_ref[pl.ds(t0, tq), :].astype(F32), qg_ref[...]).astype(BF16)
        kw = kn_ref[pl.ds(t0, win), :]
        vw = vp_ref[pl.ds(t0, win), :]
        s = lax.dot_general(qn, kw, (((1,), (1,)), ((), ())), preferred_element_type=F32) * scale
        s = s + bias_ref[...]
        valid = band & (kc >= LEFT_CHUNKS - i * (tq // CHUNK))
        s = jnp.where(valid, s, NEG_BIG)
        m = jnp.max(s, axis=-1, keepdims=True)
        p = jnp.exp(s - m)
        probs = p * (1.0 / jnp.sum(p, axis=-1, keepdims=True))
        y = jnp.dot(probs.astype(BF16), vw, preferred_element_type=F32)
        g = g_ref[pl.ds(t0, tq), :].astype(F32)
        o_ref[pl.ds(t0, tq), :] = (y * _silu(g)).astype(o_ref.dtype)
        return 0

    lax.fori_loop(0, seq // tq, q_block, 0, unroll=2)


def _ch_attn(proj, q_gain, k_gain, rel_base, *, batch, seq, n_heads, col_q, col_k, col_v, col_g):
    d = HEAD_DIM

    def spec(col):
        base = col // d
        return pl.BlockSpec((seq, d), lambda b, h: (b, base + h))

    blk = 5 * seq * d * 2 + 2 * (_CH_PAD + seq) * d * 2 + _CH_TQ * _CH_WIN * 4
    kern = functools.partial(_ch_attn_kernel, scale=HEAD_DIM ** -0.5)
    return pl.pallas_call(
        kern,
        grid=(batch, n_heads),
        in_specs=[spec(col_q), spec(col_k), spec(col_v), spec(col_g),
                  pl.BlockSpec((1, d), lambda b, h: (0, 0)),
                  pl.BlockSpec((1, d), lambda b, h: (0, 0)),
                  pl.BlockSpec((None, 1, _CH_BASE), lambda b, h: (h, 0, 0))],
        out_specs=pl.BlockSpec((seq, d), lambda b, h: (b, h)),
        out_shape=jax.ShapeDtypeStruct((batch * seq, n_heads * d), BF16),
        scratch_shapes=[pltpu.VMEM((_CH_PAD + seq, d), BF16), pltpu.VMEM((_CH_PAD + seq, d), BF16),
                        pltpu.VMEM((_CH_TQ, _CH_WIN), F32)],
        compiler_params=pltpu.CompilerParams(
            dimension_semantics=("parallel", "parallel"), vmem_limit_bytes=_vmem_limit(blk)),
        name="ch_attn",
    )(proj, proj, proj, proj, q_gain, k_gain, rel_base)


def kernel(x, norm_g, w_in, q_norm_g, k_norm_g, rel_bias, w_out):
    batch, seq, d_model = x.shape
    depth = w_in.shape[0]
    d_mix = w_out.shape[1]
    d_sb = d_mix // 2
    d_ch = d_mix - d_sb
    assert w_in.shape[2] == 4 * d_sb + 4 * d_ch and rel_bias.shape[2] == REL_CLIP + CHUNK
    assert seq % _CH_TQ == 0

    w_in_bf = w_in.astype(BF16)
    w_out_bf = w_out.astype(BF16)
    rel_base = rel_bias[:, :, _rel_base_index()][:, :, None, :]

    x2 = x.reshape(batch * seq, d_model)
    for layer in range(depth):
        proj = _in_proj(x2, norm_g[layer][None, :], w_in_bf[layer], tm=1024, tn=1024)
        ma = _sb_attn(proj, batch=batch, seq=seq, n_heads=d_sb // HEAD_DIM,
                      col_q=0, col_k=d_sb, col_v=2 * d_sb, col_g=3 * d_sb, nh=4, tq=256, tk=128)
        c0 = 4 * d_sb
        mb = _ch_attn(proj, q_norm_g[layer][None, :], k_norm_g[layer][None, :], rel_base[layer],
                      batch=batch, seq=seq, n_heads=d_ch // HEAD_DIM,
                      col_q=c0, col_k=c0 + d_ch, col_v=c0 + 2 * d_ch, col_g=c0 + 3 * d_ch)
        x2 = _out_proj(x2, ma, mb, w_out_bf[layer], tm=512)
    return x2.reshape(batch, seq, d_model)
```

```python
import functools

import jax
import jax.numpy as jnp
import numpy as np
from jax import lax
from jax.experimental import pallas as pl
from jax.experimental.pallas import tpu as pltpu

HEAD_DIM = 128
CHUNK = 64
LEFT_CHUNKS = 8
REL_CLIP = 256
NORM_EPS = 1e-6
NEG_BIG = -1e30
LOG2_E = 1.4426950408889634

V7X_LANES = 128
V7X_VMEM_BYTES = 64 * 1024 * 1024

F32 = jnp.float32
BF16 = jnp.bfloat16


def _vmem_limit(block_bytes):
    return int(min(2 * block_bytes + 24 * 1024 * 1024, V7X_VMEM_BYTES - 8 * 1024 * 1024))


def _in_proj_kernel(x_ref, g_ref, w_ref, o_ref, h_ref):
    @pl.when(pl.program_id(1) == 0)
    def _():
        x = x_ref[...]
        ms = jnp.mean(x * x, axis=-1, keepdims=True)
        h_ref[...] = (x * lax.rsqrt(ms + NORM_EPS) * g_ref[...]).astype(BF16)

    w = w_ref[...].astype(BF16)
    o_ref[...] = jnp.dot(h_ref[...], w, preferred_element_type=F32).astype(o_ref.dtype)


def _in_proj(x2, g, w_in, layer, *, tm, tn):
    m, d = x2.shape
    n = w_in.shape[2]
    blk = tm * d * 4 + d * tn * 4 + tm * tn * 2 + tm * d * 2
    return pl.pallas_call(
        _in_proj_kernel,
        grid=(m // tm, n // tn),
        in_specs=[
            pl.BlockSpec((tm, d), lambda i, j: (i, 0)),
            pl.BlockSpec((None, 1, d), lambda i, j: (layer, 0, 0)),
            pl.BlockSpec((None, d, tn), lambda i, j: (layer, 0, j)),
        ],
        out_specs=pl.BlockSpec((tm, tn), lambda i, j: (i, j)),
        out_shape=jax.ShapeDtypeStruct((m, n), BF16),
        scratch_shapes=[pltpu.VMEM((tm, d), BF16)],
        compiler_params=pltpu.CompilerParams(
            dimension_semantics=("parallel", "arbitrary"), vmem_limit_bytes=_vmem_limit(blk)),
        name="in_proj",
    )(x2, g, w_in)


def _out_proj_kernel(x_ref, ma_ref, mb_ref, w_ref, o_ref, wb_ref):
    @pl.when(pl.program_id(1) == 0)
    def _():
        wb_ref[...] = w_ref[...].astype(BF16)

    da = ma_ref.shape[1]
    acc = jnp.dot(ma_ref[...], wb_ref[:da, :], preferred_element_type=F32)
    acc = acc + jnp.dot(mb_ref[...], wb_ref[da:, :], preferred_element_type=F32)
    o_ref[...] = x_ref[...] + acc


def _out_proj(x2, ma, mb, w_out, layer, *, tm, tn):
    m, d = x2.shape
    da, db = ma.shape[1], mb.shape[1]
    blk = tm * tn * 4 * 2 + tm * (da + db) * 2 + (da + db) * tn * 4 + (da + db) * tn * 2
    return pl.pallas_call(
        _out_proj_kernel,
        grid=(d // tn, m // tm),
        in_specs=[
            pl.BlockSpec((tm, tn), lambda j, i: (i, j)),
            pl.BlockSpec((tm, da), lambda j, i: (i, 0)),
            pl.BlockSpec((tm, db), lambda j, i: (i, 0)),
            pl.BlockSpec((None, da + db, tn), lambda j, i: (layer, 0, j)),
        ],
        out_specs=pl.BlockSpec((tm, tn), lambda j, i: (i, j)),
        out_shape=jax.ShapeDtypeStruct((m, d), F32),
        scratch_shapes=[pltpu.VMEM((da + db, tn), BF16)],
        compiler_params=pltpu.CompilerParams(
            dimension_semantics=("parallel", "arbitrary"), vmem_limit_bytes=_vmem_limit(blk)),
        name="out_proj",
    )(x2, ma, mb, w_out)


def _silu(g):
    return g * (1.0 / (1.0 + jnp.exp(-g)))


def _sb_attn_kernel(q_ref, k_ref, v_ref, g_ref, o_ref, acc_ref, car_ref, *, nh, tq, tk, kb, scale):
    seq = q_ref.shape[0]
    d = HEAD_DIM
    nq = seq // tq
    kpq = tq // tk

    row = lax.broadcasted_iota(jnp.int32, (2 * tk, 2 * tk), 0) % tk
    col = lax.broadcasted_iota(jnp.int32, (2 * tk, 2 * tk), 1)
    uj = jnp.where((col >= tk) | (row > col), 1.0, 0.0).astype(BF16)
    t_minus_s = (lax.broadcasted_iota(jnp.int32, (tq, tk), 0)
                 - lax.broadcasted_iota(jnp.int32, (tq, tk), 1))

    def pairs(qs, k0s, t0, masked):
        causal = [(t_minus_s > (k0 - t0)) if masked else None for k0 in k0s]
        zs, cats = [], []
        for h in range(nh):
            for b, k0 in enumerate(k0s):
                k = k_ref[pl.ds(k0, tk), pl.ds(h * d, d)]
                z = lax.dot_general(qs[h], k, (((1,), (1,)), ((), ())),
                                    preferred_element_type=F32) * (scale * LOG2_E)
                sp = jnp.maximum(z, 0.0) + jnp.log2(1.0 + jnp.exp2(-jnp.abs(z)))
                spm = jnp.where(causal[b], sp, 0.0) if masked else sp
                hi = spm.astype(BF16)
                lo = (spm - hi.astype(F32)).astype(BF16)
                zs.append(z - sp)
                cats.append(jnp.concatenate([hi, lo], axis=1))
        ws = []
        for h in range(nh):
            car = car_ref[h]
            for b in range(len(k0s)):
                n = h * len(k0s) + b
                cs = jnp.dot(cats[n], uj, preferred_element_type=F32)
                w = jnp.exp2(zs[n] - cs[:, :tk] - car)
                if masked:
                    w = jnp.where(causal[b], w, 0.0)
                car = car + cs[:, tk:]
                ws.append(w.astype(BF16))
            car_ref[h] = car
        for h in range(nh):
            acc = acc_ref[h]
            for b, k0 in enumerate(k0s):
                v = v_ref[pl.ds(k0, tk), pl.ds(h * d, d)]
                acc = acc + jnp.dot(ws[h * len(k0s) + b], v, preferred_element_type=F32)
            acc_ref[h] = acc

    def q_block(i, _):
        t0 = pl.multiple_of(i * tq, tq)
        acc_ref[...] = jnp.zeros_like(acc_ref)
        car_ref[...] = jnp.zeros_like(car_ref)
        qs = [q_ref[pl.ds(t0, tq), pl.ds(h * d, d)] for h in range(nh)]
        pairs(qs, [pl.multiple_of(t0 + (kpq - 1 - jj) * tk, tk) for jj in range(kpq)], t0, True)

        def k_block(jj, _):
            k1 = t0 - jj * (kb * tk)
            pairs(qs, [pl.multiple_of(k1 - (b + 1) * tk, tk) for b in range(kb)], t0, False)
            return 0

        lax.fori_loop(0, i * (kpq // kb), k_block, 0)
        for h in range(nh):
            g = g_ref[pl.ds(t0, tq), pl.ds(h * d, d)].astype(F32)
            o_ref[pl.ds(t0, tq), pl.ds(h * d, d)] = (acc_ref[h] * _silu(g)).astype(o_ref.dtype)
        return 0

    lax.fori_loop(0, nq, q_block, 0)


def _sb_attn(proj, *, batch, seq, n_heads, col_q, col_k, col_v, col_g, nh, tq, tk, kb):
    wblk = nh * HEAD_DIM
    assert n_heads % nh == 0 and seq % tq == 0 and tq % (kb * tk) == 0 and tk == V7X_LANES

    def spec(col):
        base = col // wblk
        return pl.BlockSpec((seq, wblk), lambda b, hg: (b, base + hg))

    blk = 5 * seq * wblk * 2
    kern = functools.partial(_sb_attn_kernel, nh=nh, tq=tq, tk=tk, kb=kb, scale=HEAD_DIM ** -0.5)
    return pl.pallas_call(
        kern,
        grid=(batch, n_heads // nh),
        in_specs=[spec(col_q), spec(col_k), spec(col_v), spec(col_g)],
        out_specs=pl.BlockSpec((seq, wblk), lambda b, hg: (b, hg)),
        out_shape=jax.ShapeDtypeStruct((batch * seq, n_heads * HEAD_DIM), BF16),
        scratch_shapes=[pltpu.VMEM((nh, tq, HEAD_DIM), F32), pltpu.VMEM((nh, tq, HEAD_DIM), F32)],
        compiler_params=pltpu.CompilerParams(
            dimension_semantics=("parallel", "parallel"), vmem_limit_bytes=_vmem_limit(blk)),
        name="sb_attn",
    )(proj, proj, proj, proj)


_CH_TQ = 2 * CHUNK
_CH_WIN = (LEFT_CHUNKS + 2) * CHUNK
_CH_PAD = LEFT_CHUNKS * CHUNK
_CH_BASE = _CH_WIN + _CH_TQ
_CH_GROUP = 4


def _rel_base_index():
    u = np.arange(_CH_BASE)
    u = np.where(u >= _CH_WIN, u - _CH_BASE, u)
    return (np.clip(_CH_PAD - u, -(CHUNK - 1), REL_CLIP) + (CHUNK - 1)).astype(np.int32)


def _ch_attn_kernel(q_ref, k_ref, v_ref, g_ref, qg_ref, kg_ref, base_ref, o_ref,
                    kn_ref, bias_ref, *, scale):
    seq = q_ref.shape[0]
    tq, win = _CH_TQ, _CH_WIN
    nclip = _CH_PAD // tq
    nblk = seq // tq
    grp = _CH_GROUP

    def rms(x, gain):
        ms = jnp.mean(x * x, axis=-1, keepdims=True)
        return x * lax.rsqrt(ms + NORM_EPS) * gain

    kn_ref[...] = rms(k_ref[...].astype(F32), kg_ref[...]).astype(BF16)
    base = jnp.broadcast_to(base_ref[...] * LOG2_E, (tq, _CH_BASE))
    rolled = pltpu.roll(base, 0, 1, stride=1, stride_axis=0)[:, :win]
    qc = lax.broadcasted_iota(jnp.int32, (tq, win), 0) // CHUNK
    kc = lax.broadcasted_iota(jnp.int32, (tq, win), 1) // CHUNK
    bias_ref[...] = jnp.where((kc >= qc) & (kc <= qc + LEFT_CHUNKS), rolled, NEG_BIG)

    def blocks(t0s, k0s, widths, offs):
        ss = []
        for t0, k0, w, o in zip(t0s, k0s, widths, offs):
            qn = rms(q_ref[pl.ds(t0, tq), :].astype(F32), qg_ref[...]).astype(BF16)
            kw = kn_ref[pl.ds(k0, w), :]
            s = lax.dot_general(qn, kw, (((1,), (1,)), ((), ())), preferred_element_type=F32)
            ss.append(s * (scale * LOG2_E) + bias_ref[:, o:o + w])
        ps = []
        for s in ss:
            p = jnp.exp2(s - jnp.max(s, axis=-1, keepdims=True))
            ps.append((p * (1.0 / jnp.sum(p, axis=-1, keepdims=True))).astype(BF16))
        for t0, k0, w, probs in zip(t0s, k0s, widths, ps):
            y = jnp.dot(probs, v_ref[pl.ds(k0, w), :], preferred_element_type=F32)
            g = g_ref[pl.ds(t0, tq), :].astype(F32)
            o_ref[pl.ds(t0, tq), :] = (y * _silu(g)).astype(o_ref.dtype)

    blocks([i * tq for i in range(nclip)], [0] * nclip,
           [(i + 1) * tq for i in range(nclip)], [(nclip - i) * tq for i in range(nclip)])

    def group(gi, _):
        i0 = nclip + gi * grp
        t0s = [pl.multiple_of((i0 + b) * tq, tq) for b in range(grp)]
        k0s = [pl.multiple_of((i0 + b - nclip) * tq, tq) for b in range(grp)]
        blocks(t0s, k0s, [win] * grp, [0] * grp)
        return 0

    lax.fori_loop(0, (nblk - nclip) // grp, group, 0)


def _ch_attn(proj, q_gain, k_gain, rel_base, layer, *, batch, seq, n_heads, col_q, col_k, col_v, col_g):
    d = HEAD_DIM

    def spec(col):
        base = col // d
        return pl.BlockSpec((seq, d), lambda b, h: (b, base + h))

    assert _CH_PAD % _CH_TQ == 0 and (seq // _CH_TQ - _CH_PAD // _CH_TQ) % _CH_GROUP == 0
    blk = 5 * seq * d * 2 + seq * d * 2 + _CH_TQ * _CH_WIN * 4
    kern = functools.partial(_ch_attn_kernel, scale=HEAD_DIM ** -0.5)
    return pl.pallas_call(
        kern,
        grid=(batch, n_heads),
        in_specs=[spec(col_q), spec(col_k), spec(col_v), spec(col_g),
                  pl.BlockSpec((None, 1, d), lambda b, h: (layer, 0, 0)),
                  pl.BlockSpec((None, 1, d), lambda b, h: (layer, 0, 0)),
                  pl.BlockSpec((None, None, 1, _CH_BASE), lambda b, h: (layer, h, 0, 0))],
        out_specs=pl.BlockSpec((seq, d), lambda b, h: (b, h)),
        out_shape=jax.ShapeDtypeStruct((batch * seq, n_heads * d), BF16),
        scratch_shapes=[pltpu.VMEM((seq, d), BF16), pltpu.VMEM((_CH_TQ, _CH_WIN), F32)],
        compiler_params=pltpu.CompilerParams(
            dimension_semantics=("parallel", "parallel"), vmem_limit_bytes=_vmem_limit(blk)),
        name="ch_attn",
    )(proj, proj, proj, proj, q_gain, k_gain, rel_base)


def kernel(x, norm_g, w_in, q_norm_g, k_norm_g, rel_bias, w_out):
    batch, seq, d_model = x.shape
    depth = w_in.shape[0]
    d_mix = w_out.shape[1]
    d_sb = d_mix // 2
    d_ch = d_mix - d_sb
    assert w_in.shape[2] == 4 * d_sb + 4 * d_ch and rel_bias.shape[2] == REL_CLIP + CHUNK
    assert seq % _CH_TQ == 0

    rel_base = rel_bias[:, :, _rel_base_index()][:, :, None, :]
    norm_g3 = norm_g[:, None, :]
    q_gain3 = q_norm_g[:, None, :]
    k_gain3 = k_norm_g[:, None, :]

    x2 = x.reshape(batch * seq, d_model)
    for layer in range(depth):
        proj = _in_proj(x2, norm_g3, w_in, layer, tm=1024, tn=1024)
        ma = _sb_attn(proj, batch=batch, seq=seq, n_heads=d_sb // HEAD_DIM,
                      col_q=0, col_k=d_sb, col_v=2 * d_sb, col_g=3 * d_sb, nh=8, tq=256, tk=128, kb=2)
        c0 = 4 * d_sb
        mb = _ch_attn(proj, q_gain3, k_gain3, rel_base, layer,
                      batch=batch, seq=seq, n_heads=d_ch // HEAD_DIM,
                      col_q=c0, col_k=c0 + d_ch, col_v=c0 + 2 * d_ch, col_g=c0 + 3 * d_ch)
        x2 = _out_proj(x2, ma, mb, w_out, layer, tm=1024, tn=1024)
    return x2.reshape(batch, seq, d_model)
```

```python
import functools

import jax
import jax.numpy as jnp
import numpy as np
from jax import lax
from jax.experimental import pallas as pl
from jax.experimental.pallas import tpu as pltpu

HEAD_DIM = 128
CHUNK = 64
LEFT_CHUNKS = 8
REL_CLIP = 256
NORM_EPS = 1e-6
NEG_BIG = -1e30
LOG2_E = 1.4426950408889634

V7X_LANES = 128
V7X_VMEM_BYTES = 64 * 1024 * 1024

F32 = jnp.float32
BF16 = jnp.bfloat16


def _vmem_limit(block_bytes):
    return int(min(2 * block_bytes + 24 * 1024 * 1024, V7X_VMEM_BYTES - 8 * 1024 * 1024))


def _in_proj_kernel(x_ref, g_ref, w_ref, o_ref, h_ref):
    @pl.when(pl.program_id(1) == 0)
    def _():
        x = x_ref[...]
        ms = jnp.mean(x * x, axis=-1, keepdims=True)
        h_ref[...] = (x * lax.rsqrt(ms + NORM_EPS) * g_ref[...]).astype(BF16)

    w = w_ref[...].astype(BF16)
    o_ref[...] = jnp.dot(h_ref[...], w, preferred_element_type=F32).astype(o_ref.dtype)


def _in_proj(x2, g, w_in, layer, *, tm, tn):
    m, d = x2.shape
    n = w_in.shape[2]
    blk = tm * d * 4 + d * tn * 4 + tm * tn * 2 + tm * d * 2
    return pl.pallas_call(
        _in_proj_kernel,
        grid=(m // tm, n // tn),
        in_specs=[
            pl.BlockSpec((tm, d), lambda i, j: (i, 0)),
            pl.BlockSpec((None, 1, d), lambda i, j: (layer, 0, 0)),
            pl.BlockSpec((None, d, tn), lambda i, j: (layer, 0, j)),
        ],
        out_specs=pl.BlockSpec((tm, tn), lambda i, j: (i, j)),
        out_shape=jax.ShapeDtypeStruct((m, n), BF16),
        scratch_shapes=[pltpu.VMEM((tm, d), BF16)],
        compiler_params=pltpu.CompilerParams(
            dimension_semantics=("parallel", "arbitrary"), vmem_limit_bytes=_vmem_limit(blk)),
        name="in_proj",
    )(x2, g, w_in)


def _out_proj_kernel(x_ref, ma_ref, mb_ref, w_ref, o_ref, wb_ref):
    @pl.when(pl.program_id(1) == 0)
    def _():
        wb_ref[...] = w_ref[...].astype(BF16)

    da = ma_ref.shape[1]
    acc = jnp.dot(ma_ref[...], wb_ref[:da, :], preferred_element_type=F32)
    acc = acc + jnp.dot(mb_ref[...], wb_ref[da:, :], preferred_element_type=F32)
    o_ref[...] = x_ref[...] + acc


def _out_proj(x2, ma, mb, w_out, layer, *, tm, tn):
    m, d = x2.shape
    da, db = ma.shape[1], mb.shape[1]
    blk = tm * tn * 4 * 2 + tm * (da + db) * 2 + (da + db) * tn * 4 + (da + db) * tn * 2
    return pl.pallas_call(
        _out_proj_kernel,
        grid=(d // tn, m // tm),
        in_specs=[
            pl.BlockSpec((tm, tn), lambda j, i: (i, j)),
            pl.BlockSpec((tm, da), lambda j, i: (i, 0)),
            pl.BlockSpec((tm, db), lambda j, i: (i, 0)),
            pl.BlockSpec((None, da + db, tn), lambda j, i: (layer, 0, j)),
        ],
        out_specs=pl.BlockSpec((tm, tn), lambda j, i: (i, j)),
        out_shape=jax.ShapeDtypeStruct((m, d), F32),
        scratch_shapes=[pltpu.VMEM((da + db, tn), BF16)],
        compiler_params=pltpu.CompilerParams(
            dimension_semantics=("parallel", "arbitrary"), vmem_limit_bytes=_vmem_limit(blk)),
        name="out_proj",
    )(x2, ma, mb, w_out)


def _silu(g):
    return g * (1.0 / (1.0 + jnp.exp(-g)))


def _sb_sets(seq, tq):
    rows = [(i * tq, (i + 1 - jj) * tq, 0 if jj == 0 else 1, i)
            for i in range(seq // tq) for jj in range(i + 1)]
    return np.asarray(rows, np.int32).T.copy()


def _sb_attn_kernel(tbl_ref, q_ref, k_ref, v_ref, g_ref, o_ref,
                    acc_ref, car_ref, z_ref, w_ref, bm_ref, *, nh, tq, tk, scale):
    seq = q_ref.shape[0]
    d = HEAD_DIM
    nq = seq // tq
    kb = tq // tk
    nsets = tbl_ref.shape[1]

    row = lax.broadcasted_iota(jnp.int32, (2 * tk, 2 * tk), 0) % tk
    col = lax.broadcasted_iota(jnp.int32, (2 * tk, 2 * tk), 1)
    uj = jnp.where((col >= tk) | (row > col), 1.0, 0.0).astype(BF16)

    t_idx = lax.broadcasted_iota(jnp.int32, (tq, tq), 0)
    s_idx = lax.broadcasted_iota(jnp.int32, (tq, tq), 1)
    bm_ref[0] = jnp.where(s_idx < t_idx, 0.0, NEG_BIG)
    bm_ref[1] = jnp.zeros((tq, tq), F32)
    acc_ref[...] = jnp.zeros_like(acc_ref)
    car_ref[...] = jnp.zeros_like(car_ref)
    w_ref[1] = jnp.zeros(w_ref.shape[1:], BF16)

    def heads():
        return [pl.ds(h * d, d) for h in range(nh)]

    def qk_stage(n, slot):
        t0 = pl.multiple_of(tbl_ref[0, n], tq)
        k0 = pl.multiple_of(tbl_ref[1, n] - tq, tq)
        for h, hs in enumerate(heads()):
            z_ref[slot, h] = lax.dot_general(q_ref[pl.ds(t0, tq), hs], k_ref[pl.ds(k0, tq), hs],
                                             (((1,), (1,)), ((), ())), preferred_element_type=F32)

    def pv_stage(n, slot):
        k0 = pl.multiple_of(tbl_ref[1, n] - tq, tq)
        qi = tbl_ref[3, n]
        for h, hs in enumerate(heads()):
            acc_ref[qi, h] += jnp.dot(w_ref[slot, h], v_ref[pl.ds(k0, tq), hs], preferred_element_type=F32)

    def mid_stage(n, zslot, wslot):
        kind = tbl_ref[2, n]
        qi = tbl_ref[3, n]
        zss, cats = [], []
        for h in range(nh):
            for b in range(kb):
                cols = pl.ds((kb - 1 - b) * tk, tk)
                z = z_ref[zslot, h, :, cols] * (scale * LOG2_E) + bm_ref[kind, :, cols]
                sp = jnp.maximum(z, 0.0) + jnp.log2(1.0 + jnp.exp2(-jnp.abs(z)))
                hi = sp.astype(BF16)
                lo = (sp - hi.astype(F32)).astype(BF16)
                zss.append(z - sp)
                cats.append(jnp.concatenate([hi, lo], axis=1))
        for h in range(nh):
            car = car_ref[qi, h]
            for b in range(kb):
                c = h * kb + b
                cs = jnp.dot(cats[c], uj, preferred_element_type=F32)
                w = jnp.exp2(zss[c] - cs[:, :tk] - car)
                car = car + cs[:, tk:]
                w_ref[wslot, h, :, pl.ds((kb - 1 - b) * tk, tk)] = w.astype(BF16)
            car_ref[qi, h] = car

    qk_stage(0, 0)

    def two_sets(m, _):
        n = 2 * m
        qk_stage(n + 1, 1)
        pv_stage(jnp.maximum(n - 1, 0), 1)
        mid_stage(n, 0, 0)
        qk_stage(jnp.minimum(n + 2, nsets - 1), 0)
        pv_stage(n, 0)
        mid_stage(n + 1, 1, 1)
        return 0

    lax.fori_loop(0, nsets // 2, two_sets, 0)
    pv_stage(nsets - 1, 1)

    def finish(i, _):
        t0 = pl.multiple_of(i * tq, tq)
        for h, hs in enumerate(heads()):
            g = g_ref[pl.ds(t0, tq), hs].astype(F32)
            o_ref[pl.ds(t0, tq), hs] = (acc_ref[i, h] * _silu(g)).astype(o_ref.dtype)
        return 0

    lax.fori_loop(0, nq, finish, 0)


def _sb_attn(proj, *, batch, seq, n_heads, col_q, col_k, col_v, col_g, nh, tq, tk):
    wblk = nh * HEAD_DIM
    nq = seq // tq
    tbl = _sb_sets(seq, tq)
    assert n_heads % nh == 0 and seq % tq == 0 and tq % tk == 0 and tk == V7X_LANES
    assert tbl.shape[1] % 2 == 0

    def spec(col):
        base = col // wblk
        return pl.BlockSpec((seq, wblk), lambda b, hg, tbl_ref: (b, base + hg))

    blk = 5 * seq * wblk * 2 + 2 * nq * nh * tq * HEAD_DIM * 4 + nh * tq * tq * (2 * 4 + 2 * 2)
    kern = functools.partial(_sb_attn_kernel, nh=nh, tq=tq, tk=tk, scale=HEAD_DIM ** -0.5)
    return pl.pallas_call(
        kern,
        grid_spec=pltpu.PrefetchScalarGridSpec(
            num_scalar_prefetch=1,
            grid=(batch, n_heads // nh),
            in_specs=[spec(col_q), spec(col_k), spec(col_v), spec(col_g)],
            out_specs=pl.BlockSpec((seq, wblk), lambda b, hg, tbl_ref: (b, hg)),
            scratch_shapes=[pltpu.VMEM((nq, nh, tq, HEAD_DIM), F32), pltpu.VMEM((nq, nh, tq, HEAD_DIM), F32),
                            pltpu.VMEM((2, nh, tq, tq), F32), pltpu.VMEM((2, nh, tq, tq), BF16),
                            pltpu.VMEM((2, tq, tq), F32)]),
        out_shape=jax.ShapeDtypeStruct((batch * seq, n_heads * HEAD_DIM), BF16),
        compiler_params=pltpu.CompilerParams(
            dimension_semantics=("parallel", "parallel"), vmem_limit_bytes=_vmem_limit(blk)),
        name="sb_attn",
    )(jnp.asarray(tbl), proj, proj, proj, proj)


_CH_TQ = 2 * CHUNK
_CH_WIN = (LEFT_CHUNKS + 2) * CHUNK
_CH_PAD = LEFT_CHUNKS * CHUNK
_CH_BASE = _CH_WIN + _CH_TQ
_CH_GROUP = 6


def _rel_base_index():
    u = np.arange(_CH_BASE)
    u = np.where(u >= _CH_WIN, u - _CH_BASE, u)
    return (np.clip(_CH_PAD - u, -(CHUNK - 1), REL_CLIP) + (CHUNK - 1)).astype(np.int32)


def _ch_attn_kernel(q_ref, k_ref, v_ref, g_ref, qg_ref, kg_ref, base_ref, o_ref,
                    kn_ref, bias_ref, *, scale):
    seq = q_ref.shape[0]
    tq, win = _CH_TQ, _CH_WIN
    nclip = _CH_PAD // tq
    nblk = seq // tq
    grp = _CH_GROUP

    def rms(x, gain):
        ms = jnp.mean(x * x, axis=-1, keepdims=True)
        return x * lax.rsqrt(ms + NORM_EPS) * gain

    kn_ref[...] = rms(k_ref[...].astype(F32), kg_ref[...]).astype(BF16)
    base = jnp.broadcast_to(base_ref[...] * LOG2_E, (tq, _CH_BASE))
    rolled = pltpu.roll(base, 0, 1, stride=1, stride_axis=0)[:, :win]
    qc = lax.broadcasted_iota(jnp.int32, (tq, win), 0) // CHUNK
    kc = lax.broadcasted_iota(jnp.int32, (tq, win), 1) // CHUNK
    bias_ref[...] = jnp.where((kc >= qc) & (kc <= qc + LEFT_CHUNKS), rolled, NEG_BIG)

    def blocks(t0s, k0s, widths, offs):
        ss = []
        for t0, k0, w, o in zip(t0s, k0s, widths, offs):
            qn = rms(q_ref[pl.ds(t0, tq), :].astype(F32), qg_ref[...]).astype(BF16)
            kw = kn_ref[pl.ds(k0, w), :]
            s = lax.dot_general(qn, kw, (((1,), (1,)), ((), ())), preferred_element_type=F32)
            ss.append(s * (scale * LOG2_E) + bias_ref[:, o:o + w])
        ps = []
        for s in ss:
            p = jnp.exp2(s - jnp.max(s, axis=-1, keepdims=True))
            ps.append((p * (1.0 / jnp.sum(p, axis=-1, keepdims=True))).astype(BF16))
        for t0, k0, w, probs in zip(t0s, k0s, widths, ps):
            y = jnp.dot(probs, v_ref[pl.ds(k0, w), :], preferred_element_type=F32)
            g = g_ref[pl.ds(t0, tq), :].astype(F32)
            o_ref[pl.ds(t0, tq), :] = (y * _silu(g)).astype(o_ref.dtype)

    blocks([i * tq for i in range(nclip)], [0] * nclip,
           [(i + 1) * tq for i in range(nclip)], [(nclip - i) * tq for i in range(nclip)])

    def group(gi, _):
        i0 = nclip + gi * grp
        t0s = [pl.multiple_of((i0 + b) * tq, tq) for b in range(grp)]
        k0s = [pl.multiple_of((i0 + b - nclip) * tq, tq) for b in range(grp)]
        blocks(t0s, k0s, [win] * grp, [0] * grp)
        return 0

    lax.fori_loop(0, (nblk - nclip) // grp, group, 0)


def _ch_attn(proj, q_gain, k_gain, rel_base, layer, *, batch, seq, n_heads, col_q, col_k, col_v, col_g):
    d = HEAD_DIM

    def spec(col):
        base = col // d
        return pl.BlockSpec((seq, d), lambda b, h: (b, base + h))

    assert _CH_PAD % _CH_TQ == 0 and (seq // _CH_TQ - _CH_PAD // _CH_TQ) % _CH_GROUP == 0
    blk = 5 * seq * d * 2 + seq * d * 2 + _CH_TQ * _CH_WIN * 4
    kern = functools.partial(_ch_attn_kernel, scale=HEAD_DIM ** -0.5)
    return pl.pallas_call(
        kern,
        grid=(batch, n_heads),
        in_specs=[spec(col_q), spec(col_k), spec(col_v), spec(col_g),
                  pl.BlockSpec((None, 1, d), lambda b, h: (layer, 0, 0)),
                  pl.BlockSpec((None, 1, d), lambda b, h: (layer, 0, 0)),
                  pl.BlockSpec((None, None, 1, _CH_BASE), lambda b, h: (layer, h, 0, 0))],
        out_specs=pl.BlockSpec((seq, d), lambda b, h: (b, h)),
        out_shape=jax.ShapeDtypeStruct((batch * seq, n_heads * d), BF16),
        scratch_shapes=[pltpu.VMEM((seq, d), BF16), pltpu.VMEM((_CH_TQ, _CH_WIN), F32)],
        compiler_params=pltpu.CompilerParams(
            dimension_semantics=("parallel", "parallel"), vmem_limit_bytes=_vmem_limit(blk)),
        name="ch_attn",
    )(proj, proj, proj, proj, q_gain, k_gain, rel_base)


def kernel(x, norm_g, w_in, q_norm_g, k_norm_g, rel_bias, w_out):
    batch, seq, d_model = x.shape
    depth = w_in.shape[0]
    d_mix = w_out.shape[1]
    d_sb = d_mix // 2
    d_ch = d_mix - d_sb
    assert w_in.shape[2] == 4 * d_sb + 4 * d_ch and rel_bias.shape[2] == REL_CLIP + CHUNK
    assert seq % _CH_TQ == 0

    rel_base = rel_bias[:, :, _rel_base_index()][:, :, None, :]
    norm_g3 = norm_g[:, None, :]
    q_gain3 = q_norm_g[:, None, :]
    k_gain3 = k_norm_g[:, None, :]

    x2 = x.reshape(batch * seq, d_model)
    for layer in range(depth):
        proj = _in_proj(x2, norm_g3, w_in, layer, tm=1024, tn=1024)
        ma = _sb_attn(proj, batch=batch, seq=seq, n_heads=d_sb // HEAD_DIM,
                      col_q=0, col_k=d_sb, col_v=2 * d_sb, col_g=3 * d_sb, nh=4, tq=256, tk=128)
        c0 = 4 * d_sb
        mb = _ch_attn(proj, q_gain3, k_gain3, rel_base, layer,
                      batch=batch, seq=seq, n_heads=d_ch // HEAD_DIM,
                      col_q=c0, col_k=c0 + d_ch, col_v=c0 + 2 * d_ch, col_g=c0 + 3 * d_ch)
        x2 = _out_proj(x2, ma, mb, w_out, layer, tm=1024, tn=1024)
    return x2.reshape(batch, seq, d_model)
```

```python
import functools

import jax
import jax.numpy as jnp
import numpy as np
from jax import lax
from jax.experimental import pallas as pl
from jax.experimental.pallas import tpu as pltpu

HEAD_DIM = 128
CHUNK = 64
LEFT_CHUNKS = 8
REL_CLIP = 256
NORM_EPS = 1e-6
NEG_BIG = -1e30
LOG2_E = 1.4426950408889634

V7X_LANES = 128
V7X_VMEM_BYTES = 64 * 1024 * 1024

F32 = jnp.float32
BF16 = jnp.bfloat16


def _vmem_limit(block_bytes):
    return int(min(2 * block_bytes + 24 * 1024 * 1024, V7X_VMEM_BYTES - 8 * 1024 * 1024))


def _in_proj_kernel(x_ref, g_ref, w_ref, o_ref, h_ref):
    @pl.when(pl.program_id(1) == 0)
    def _():
        x = x_ref[...]
        ms = jnp.mean(x * x, axis=-1, keepdims=True)
        h_ref[...] = (x * lax.rsqrt(ms + NORM_EPS) * g_ref[...]).astype(BF16)

    w = w_ref[...].astype(BF16)
    o_ref[...] = jnp.dot(h_ref[...], w, preferred_element_type=F32).astype(o_ref.dtype)


def _in_proj(x2, g, w_in, layer, *, tm, tn):
    m, d = x2.shape
    n = w_in.shape[2]
    blk = tm * d * 4 + d * tn * 4 + tm * tn * 2 + tm * d * 2
    return pl.pallas_call(
        _in_proj_kernel,
        grid=(m // tm, n // tn),
        in_specs=[
            pl.BlockSpec((tm, d), lambda i, j: (i, 0)),
            pl.BlockSpec((None, 1, d), lambda i, j: (layer, 0, 0)),
            pl.BlockSpec((None, d, tn), lambda i, j: (layer, 0, j)),
        ],
        out_specs=pl.BlockSpec((tm, tn), lambda i, j: (i, j)),
        out_shape=jax.ShapeDtypeStruct((m, n), BF16),
        scratch_shapes=[pltpu.VMEM((tm, d), BF16)],
        compiler_params=pltpu.CompilerParams(
            dimension_semantics=("parallel", "arbitrary"), vmem_limit_bytes=_vmem_limit(blk)),
        name="in_proj",
    )(x2, g, w_in)


def _out_proj_kernel(x_ref, ma_ref, mb_ref, w_ref, o_ref, wb_ref):
    @pl.when(pl.program_id(1) == 0)
    def _():
        wb_ref[...] = w_ref[...].astype(BF16)

    da = ma_ref.shape[1]
    acc = jnp.dot(ma_ref[...], wb_ref[:da, :], preferred_element_type=F32)
    acc = acc + jnp.dot(mb_ref[...], wb_ref[da:, :], preferred_element_type=F32)
    o_ref[...] = x_ref[...] + acc


def _out_proj(x2, ma, mb, w_out, layer, *, tm, tn):
    m, d = x2.shape
    da, db = ma.shape[1], mb.shape[1]
    blk = tm * tn * 4 * 2 + tm * (da + db) * 2 + (da + db) * tn * 4 + (da + db) * tn * 2
    return pl.pallas_call(
        _out_proj_kernel,
        grid=(d // tn, m // tm),
        in_specs=[
            pl.BlockSpec((tm, tn), lambda j, i: (i, j)),
            pl.BlockSpec((tm, da), lambda j, i: (i, 0)),
            pl.BlockSpec((tm, db), lambda j, i: (i, 0)),
            pl.BlockSpec((None, da + db, tn), lambda j, i: (layer, 0, j)),
        ],
        out_specs=pl.BlockSpec((tm, tn), lambda j, i: (i, j)),
        out_shape=jax.ShapeDtypeStruct((m, d), F32),
        scratch_shapes=[pltpu.VMEM((da + db, tn), BF16)],
        compiler_params=pltpu.CompilerParams(
            dimension_semantics=("parallel", "arbitrary"), vmem_limit_bytes=_vmem_limit(blk)),
        name="out_proj",
    )(x2, ma, mb, w_out)


def _silu(g):
    return g * (1.0 / (1.0 + jnp.exp(-g)))


def _sb_sets(seq, tq):
    rows = [(i * tq, (i - jj) * tq, i) for i in range(seq // tq) for jj in range(1, i + 1)]
    return np.asarray(rows, np.int32).T.copy()


def _sb_attn_kernel(tbl_ref, q_ref, k_ref, v_ref, g_ref, o_ref,
                    acc_ref, car_ref, z_ref, w_ref, bm_ref, *, nh, tq, tk, scale):
    seq = q_ref.shape[0]
    d = HEAD_DIM
    nq = seq // tq
    nsets = tbl_ref.shape[1]
    c2 = scale * LOG2_E

    row = lax.broadcasted_iota(jnp.int32, (2 * tk, 2 * tk), 0) % tk
    col = lax.broadcasted_iota(jnp.int32, (2 * tk, 2 * tk), 1)
    uj = jnp.where((col >= tk) | (row > col), 1.0, 0.0).astype(BF16)
    t_idx = lax.broadcasted_iota(jnp.int32, (tk, tk), 0)
    s_idx = lax.broadcasted_iota(jnp.int32, (tk, tk), 1)
    bm_ref[...] = jnp.where(s_idx < t_idx, 0.0, NEG_BIG)
    w_ref[...] = jnp.zeros_like(w_ref)

    heads = [pl.ds(h * d, d) for h in range(nh)]

    def split(z):
        sp = jnp.maximum(z, 0.0) + jnp.log2(1.0 + jnp.exp2(-jnp.abs(z)))
        hi = sp.astype(BF16)
        lo = (sp - hi.astype(F32)).astype(BF16)
        return z - sp, jnp.concatenate([hi, lo], axis=1)

    def qk_stage(t0, k0, slot):
        for h, hs in enumerate(heads):
            z_ref[slot, h] = lax.dot_general(q_ref[pl.ds(t0, tq), hs], k_ref[pl.ds(k0, tq), hs],
                                             (((1,), (1,)), ((), ())), preferred_element_type=F32)

    def pv_stage(k0, qi, slot, init):
        for h, hs in enumerate(heads):
            y = jnp.dot(w_ref[slot, h], v_ref[pl.ds(k0, tq), hs], preferred_element_type=F32)
            if init:
                acc_ref[qi, h] = y
            else:
                acc_ref[qi, h] += y

    def mid_diag(qi, zslot, wslot):
        parts = []
        for h in range(nh):
            zl = z_ref[zslot, h, :, :tk] * c2
            zl = jnp.concatenate([zl[:tk] + bm_ref[...], zl[tk:]], axis=0)
            zr = z_ref[zslot, h, tk:, tk:] * c2 + bm_ref[...]
            parts.append(split(zl) + split(zr))
        for h in range(nh):
            zs_l, cat_l, zs_r, cat_r = parts[h]
            cs_r = jnp.dot(cat_r, uj, preferred_element_type=F32)
            cs_l = jnp.dot(cat_l, uj, preferred_element_type=F32)
            tot_r = cs_r[:, tk:]
            w_ref[wslot, h, tk:, tk:] = jnp.exp2(zs_r - cs_r[:, :tk]).astype(BF16)
            w_ref[wslot, h, :tk, :tk] = jnp.exp2(zs_l[:tk] - cs_l[:tk, :tk]).astype(BF16)
            w_ref[wslot, h, tk:, :tk] = jnp.exp2(zs_l[tk:] - cs_l[tk:, :tk] - tot_r).astype(BF16)
            car_ref[qi, h, :tk] = cs_l[:tk, tk:]
            car_ref[qi, h, tk:] = tot_r + cs_l[tk:, tk:]

    def mid_full(qi, zslot, wslot):
        parts = []
        for h in range(nh):
            for b in range(2):
                parts.append(split(z_ref[zslot, h, :, pl.ds((1 - b) * tk, tk)] * c2))
        for h in range(nh):
            car = car_ref[qi, h]
            for b in range(2):
                zs, cat = parts[2 * h + b]
                cs = jnp.dot(cat, uj, preferred_element_type=F32)
                w_ref[wslot, h, :, pl.ds((1 - b) * tk, tk)] = jnp.exp2(zs - cs[:, :tk] - car).astype(BF16)
                car = car + cs[:, tk:]
            car_ref[qi, h] = car

    def pipeline(count, qk, pv, mid):
        qk(0, 0)

        def two_sets(m, _):
            n = 2 * m
            qk(n + 1, 1)
            pv(n - 1, 1)
            mid(n, 0, 0)
            qk(jnp.minimum(n + 2, count - 1), 0)
            pv(n, 0)
            mid(n + 1, 1, 1)
            return 0

        lax.fori_loop(0, count // 2, two_sets, 0)
        pv(count - 1, 1)

    def row0(i):
        return pl.multiple_of(i * tq, tq)

    def spare(n, qi):
        return jnp.where(n < 0, nq, qi)

    pipeline(nq,
             lambda i, slot: qk_stage(row0(i), row0(i), slot),
             lambda i, slot: pv_stage(row0(jnp.maximum(i, 0)), spare(i, i), slot, True),
             mid_diag)

    def key0(n):
        return pl.multiple_of(tbl_ref[1, jnp.maximum(n, 0)], tq)

    pipeline(nsets,
             lambda n, slot: qk_stage(pl.multiple_of(tbl_ref[0, n], tq), key0(n), slot),
             lambda n, slot: pv_stage(key0(n), spare(n, tbl_ref[2, jnp.maximum(n, 0)]), slot, False),
             lambda n, zslot, wslot: mid_full(tbl_ref[2, n], zslot, wslot))

    def finish(i, _):
        t0 = row0(i)
        for h, hs in enumerate(heads):
            g = g_ref[pl.ds(t0, tq), hs].astype(F32)
            o_ref[pl.ds(t0, tq), hs] = (acc_ref[i, h] * _silu(g)).astype(o_ref.dtype)
        return 0

    lax.fori_loop(0, nq, finish, 0)


def _sb_attn(proj, *, batch, seq, n_heads, col_q, col_k, col_v, col_g, nh, tq, tk):
    wblk = nh * HEAD_DIM
    nq = seq // tq
    tbl = _sb_sets(seq, tq)
    assert n_heads % nh == 0 and seq % tq == 0 and tq == 2 * tk and tk == V7X_LANES
    assert nq % 2 == 0 and tbl.shape[1] % 2 == 0

    def spec(col):
        base = col // wblk
        return pl.BlockSpec((seq, wblk), lambda b, hg, tbl_ref: (b, base + hg))

    blk = 5 * seq * wblk * 2 + 2 * nq * nh * tq * HEAD_DIM * 4 + nh * tq * tq * (2 * 4 + 2 * 2)
    kern = functools.partial(_sb_attn_kernel, nh=nh, tq=tq, tk=tk, scale=HEAD_DIM ** -0.5)
    return pl.pallas_call(
        kern,
        grid_spec=pltpu.PrefetchScalarGridSpec(
            num_scalar_prefetch=1,
            grid=(batch, n_heads // nh),
            in_specs=[spec(col_q), spec(col_k), spec(col_v), spec(col_g)],
            out_specs=pl.BlockSpec((seq, wblk), lambda b, hg, tbl_ref: (b, hg)),
            scratch_shapes=[pltpu.VMEM((nq + 1, nh, tq, HEAD_DIM), F32), pltpu.VMEM((nq, nh, tq, HEAD_DIM), F32),
                            pltpu.VMEM((2, nh, tq, tq), F32), pltpu.VMEM((2, nh, tq, tq), BF16),
                            pltpu.VMEM((tk, tk), F32)]),
        out_shape=jax.ShapeDtypeStruct((batch * seq, n_heads * HEAD_DIM), BF16),
        compiler_params=pltpu.CompilerParams(
            dimension_semantics=("parallel", "parallel"), vmem_limit_bytes=_vmem_limit(blk)),
        name="sb_attn",
    )(jnp.asarray(tbl), proj, proj, proj, proj)


_CH_TQ = 2 * CHUNK
_CH_WIN = (LEFT_CHUNKS + 2) * CHUNK
_CH_PAD = LEFT_CHUNKS * CHUNK
_CH_BASE = _CH_WIN + _CH_TQ


def _rel_base_index():
    u = np.arange(_CH_BASE)
    u = np.where(u >= _CH_WIN, u - _CH_BASE, u)
    return (np.clip(_CH_PAD - u, -(CHUNK - 1), REL_CLIP) + (CHUNK - 1)).astype(np.int32)


def _ch_attn_kernel(q_ref, k_ref, v_ref, g_ref, qg_ref, kg_ref, base_ref, o_ref,
                    kn_ref, bias_ref, *, scale):
    seq = q_ref.shape[0]
    tq, win = _CH_TQ, _CH_WIN
    nclip = _CH_PAD // tq
    nblk = seq // tq

    def rms(x, gain):
        ms = jnp.mean(x * x, axis=-1, keepdims=True)
        return x * lax.rsqrt(ms + NORM_EPS) * gain

    kn_ref[...] = rms(k_ref[...].astype(F32), kg_ref[...]).astype(BF16)
    base = jnp.broadcast_to(base_ref[...] * LOG2_E, (tq, _CH_BASE))
    rolled = pltpu.roll(base, 0, 1, stride=1, stride_axis=0)[:, :win]
    qc = lax.broadcasted_iota(jnp.int32, (tq, win), 0) // CHUNK
    kc = lax.broadcasted_iota(jnp.int32, (tq, win), 1) // CHUNK
    bias_ref[...] = jnp.where((kc >= qc) & (kc <= qc + LEFT_CHUNKS), rolled, NEG_BIG)

    def blocks(t0s, k0s, widths, offs):
        def scores(n):
            qn = rms(q_ref[pl.ds(t0s[n], tq), :].astype(F32), qg_ref[...]).astype(BF16)
            kw = kn_ref[pl.ds(k0s[n], widths[n]), :]
            s = lax.dot_general(qn, kw, (((1,), (1,)), ((), ())), preferred_element_type=F32)
            return s * (scale * LOG2_E) + bias_ref[:, offs[n]:offs[n] + widths[n]]

        def softmax(s):
            p = jnp.exp2(s - jnp.max(s, axis=-1, keepdims=True))
            return (p * (1.0 / jnp.sum(p, axis=-1, keepdims=True))).astype(BF16)

        def output(n, probs):
            y = jnp.dot(probs, v_ref[pl.ds(k0s[n], widths[n]), :], preferred_element_type=F32)
            g = g_ref[pl.ds(t0s[n], tq), :].astype(F32)
            o_ref[pl.ds(t0s[n], tq), :] = (y * _silu(g)).astype(o_ref.dtype)

        ss = [scores(n) for n in range(len(t0s))]
        ps = [softmax(s) for s in ss]
        for n, probs in enumerate(ps):
            output(n, probs)

    for idx in (range(nclip), range(nclip, nblk)):
        blocks([i * tq for i in idx],
               [max(i - nclip, 0) * tq for i in idx],
               [min(i + 1, nclip + 1) * tq for i in idx],
               [max(nclip - i, 0) * tq for i in idx])


def _ch_attn(proj, q_gain, k_gain, rel_base, layer, *, batch, seq, n_heads, col_q, col_k, col_v, col_g):
    d = HEAD_DIM

    def spec(col):
        base = col // d
        return pl.BlockSpec((seq, d), lambda b, h: (b, base + h))

    assert _CH_PAD % _CH_TQ == 0 and seq % _CH_TQ == 0
    blk = 5 * seq * d * 2 + seq * d * 2 + _CH_TQ * _CH_WIN * 4
    kern = functools.partial(_ch_attn_kernel, scale=HEAD_DIM ** -0.5)
    return pl.pallas_call(
        kern,
        grid=(batch, n_heads),
        in_specs=[spec(col_q), spec(col_k), spec(col_v), spec(col_g),
                  pl.BlockSpec((None, 1, d), lambda b, h: (layer, 0, 0)),
                  pl.BlockSpec((None, 1, d), lambda b, h: (layer, 0, 0)),
                  pl.BlockSpec((None, None, 1, _CH_BASE), lambda b, h: (layer, h, 0, 0))],
        out_specs=pl.BlockSpec((seq, d), lambda b, h: (b, h)),
        out_shape=jax.ShapeDtypeStruct((batch * seq, n_heads * d), BF16),
        scratch_shapes=[pltpu.VMEM((seq, d), BF16), pltpu.VMEM((_CH_TQ, _CH_WIN), F32)],
        compiler_params=pltpu.CompilerParams(
            dimension_semantics=("parallel", "parallel"), vmem_limit_bytes=_vmem_limit(blk)),
        name="ch_attn",
    )(proj, proj, proj, proj, q_gain, k_gain, rel_base)


def kernel(x, norm_g, w_in, q_norm_g, k_norm_g, rel_bias, w_out):
    batch, seq, d_model = x.shape
    depth = w_in.shape[0]
    d_mix = w_out.shape[1]
    d_sb = d_mix // 2
    d_ch = d_mix - d_sb
    assert w_in.shape[2] == 4 * d_sb + 4 * d_ch and rel_bias.shape[2] == REL_CLIP + CHUNK
    assert seq % _CH_TQ == 0

    rel_base = rel_bias[:, :, _rel_base_index()][:, :, None, :]
    norm_g3 = norm_g[:, None, :]
    q_gain3 = q_norm_g[:, None, :]
    k_gain3 = k_norm_g[:, None, :]

    x2 = x.reshape(batch * seq, d_model)
    for layer in range(depth):
        proj = _in_proj(x2, norm_g3, w_in, layer, tm=1024, tn=1024)
        ma = _sb_attn(proj, batch=batch, seq=seq, n_heads=d_sb // HEAD_DIM,
                      col_q=0, col_k=d_sb, col_v=2 * d_sb, col_g=3 * d_sb, nh=4, tq=256, tk=128)
        c0 = 4 * d_sb
        mb = _ch_attn(proj, q_gain3, k_gain3, rel_base, layer,
                      batch=batch, seq=seq, n_heads=d_ch // HEAD_DIM,
                      col_q=c0, col_k=c0 + d_ch, col_v=c0 + 2 * d_ch, col_g=c0 + 3 * d_ch)
        x2 = _out_proj(x2, ma, mb, w_out, layer, tm=1024, tn=1024)
    return x2.reshape(batch, seq, d_model)
```

```python
import functools

import jax
import jax.numpy as jnp
import numpy as np
from jax import lax
from jax.experimental import pallas as pl
from jax.experimental.pallas import tpu as pltpu

HEAD_DIM = 128
CHUNK = 64
LEFT_CHUNKS = 8
REL_CLIP = 256
NORM_EPS = 1e-6
NEG_BIG = -1e30
LOG2_E = 1.4426950408889634

V7X_LANES = 128
V7X_VMEM_BYTES = 64 * 1024 * 1024

F32 = jnp.float32
BF16 = jnp.bfloat16


def _vmem_limit(block_bytes):
    return int(min(2 * block_bytes + 24 * 1024 * 1024, V7X_VMEM_BYTES - 8 * 1024 * 1024))


def _in_proj_kernel(x_ref, g_ref, w_ref, cs_ref, o_ref, h_ref):
    @pl.when(pl.program_id(1) == 0)
    def _():
        x = x_ref[...]
        ms = jnp.mean(x * x, axis=-1, keepdims=True)
        h_ref[...] = (x * lax.rsqrt(ms + NORM_EPS) * g_ref[...]).astype(BF16)

    w = w_ref[...].astype(BF16)
    o_ref[...] = (jnp.dot(h_ref[...], w, preferred_element_type=F32) * cs_ref[...]).astype(o_ref.dtype)


def _in_proj(x2, g, w_in, col_scale, layer, *, tm, tn):
    m, d = x2.shape
    n = w_in.shape[2]
    blk = tm * d * 4 + d * tn * 4 + tm * tn * 2 + tm * d * 2
    return pl.pallas_call(
        _in_proj_kernel,
        grid=(m // tm, n // tn),
        in_specs=[
            pl.BlockSpec((tm, d), lambda i, j: (i, 0)),
            pl.BlockSpec((None, 1, d), lambda i, j: (layer, 0, 0)),
            pl.BlockSpec((None, d, tn), lambda i, j: (layer, 0, j)),
            pl.BlockSpec((1, tn), lambda i, j: (0, j)),
        ],
        out_specs=pl.BlockSpec((tm, tn), lambda i, j: (i, j)),
        out_shape=jax.ShapeDtypeStruct((m, n), BF16),
        scratch_shapes=[pltpu.VMEM((tm, d), BF16)],
        compiler_params=pltpu.CompilerParams(
            dimension_semantics=("parallel", "arbitrary"), vmem_limit_bytes=_vmem_limit(blk)),
        name="in_proj",
    )(x2, g, w_in, col_scale)


def _out_proj_kernel(x_ref, ma_ref, mb_ref, w_ref, o_ref, wb_ref):
    @pl.when(pl.program_id(1) == 0)
    def _():
        wb_ref[...] = w_ref[...].astype(BF16)

    da = ma_ref.shape[1]
    acc = jnp.dot(ma_ref[...], wb_ref[:da, :], preferred_element_type=F32)
    acc = acc + jnp.dot(mb_ref[...], wb_ref[da:, :], preferred_element_type=F32)
    o_ref[...] = x_ref[...] + acc


def _out_proj(x2, ma, mb, w_out, layer, *, tm, tn):
    m, d = x2.shape
    da, db = ma.shape[1], mb.shape[1]
    blk = tm * tn * 4 * 2 + tm * (da + db) * 2 + (da + db) * tn * 4 + (da + db) * tn * 2
    return pl.pallas_call(
        _out_proj_kernel,
        grid=(d // tn, m // tm),
        in_specs=[
            pl.BlockSpec((tm, tn), lambda j, i: (i, j)),
            pl.BlockSpec((tm, da), lambda j, i: (i, 0)),
            pl.BlockSpec((tm, db), lambda j, i: (i, 0)),
            pl.BlockSpec((None, da + db, tn), lambda j, i: (layer, 0, j)),
        ],
        out_specs=pl.BlockSpec((tm, tn), lambda j, i: (i, j)),
        out_shape=jax.ShapeDtypeStruct((m, d), F32),
        scratch_shapes=[pltpu.VMEM((da + db, tn), BF16)],
        compiler_params=pltpu.CompilerParams(
            dimension_semantics=("parallel", "arbitrary"), vmem_limit_bytes=_vmem_limit(blk)),
        name="out_proj",
    )(x2, ma, mb, w_out)


def _silu(g):
    return g * (1.0 / (1.0 + jnp.exp(-g)))


def _sb_sets(seq, tq):
    rows = [(i * tq, (i - jj) * tq, i) for i in range(seq // tq) for jj in range(1, i + 1)]
    return np.asarray(rows, np.int32).T.copy()


def _sb_attn_kernel(tbl_ref, q_ref, k_ref, v_ref, g_ref, o_ref,
                    acc_ref, car_ref, z_ref, w_ref, bm_ref, *, nh, tq, tk):
    seq = q_ref.shape[0]
    d = HEAD_DIM
    nq = seq // tq
    nsets = tbl_ref.shape[1]

    row = lax.broadcasted_iota(jnp.int32, (2 * tk, 2 * tk), 0) % tk
    col = lax.broadcasted_iota(jnp.int32, (2 * tk, 2 * tk), 1)
    uj = jnp.where((col >= tk) | (row > col), 1.0, 0.0).astype(BF16)
    t_idx = lax.broadcasted_iota(jnp.int32, (tk, tk), 0)
    s_idx = lax.broadcasted_iota(jnp.int32, (tk, tk), 1)
    bm_ref[...] = jnp.where(s_idx < t_idx, 0.0, NEG_BIG)
    w_ref[...] = jnp.zeros_like(w_ref)

    heads = [pl.ds(h * d, d) for h in range(nh)]

    def split(z):
        sp = jnp.maximum(z, 0.0) + jnp.log2(1.0 + jnp.exp2(-jnp.abs(z)))
        hi = sp.astype(BF16)
        lo = (sp - hi.astype(F32)).astype(BF16)
        return z - sp, jnp.concatenate([hi, lo], axis=1)

    def qk_stage(t0, k0, slot):
        for h, hs in enumerate(heads):
            z_ref[slot, h] = lax.dot_general(q_ref[pl.ds(t0, tq), hs], k_ref[pl.ds(k0, tq), hs],
                                             (((1,), (1,)), ((), ())), preferred_element_type=F32)

    def pv_stage(k0, qi, slot, init):
        for h, hs in enumerate(heads):
            y = jnp.dot(w_ref[slot, h], v_ref[pl.ds(k0, tq), hs], preferred_element_type=F32)
            if init:
                acc_ref[qi, h] = y
            else:
                acc_ref[qi, h] += y

    def mid_diag(qi, zslot, wslot):
        parts = []
        for h in range(nh):
            zl = z_ref[zslot, h, :, :tk]
            zl = jnp.concatenate([zl[:tk] + bm_ref[...], zl[tk:]], axis=0)
            zr = z_ref[zslot, h, tk:, tk:] + bm_ref[...]
            parts.append(split(zl) + split(zr))
        for h in range(nh):
            zs_l, cat_l, zs_r, cat_r = parts[h]
            cs_r = jnp.dot(cat_r, uj, preferred_element_type=F32)
            cs_l = jnp.dot(cat_l, uj, preferred_element_type=F32)
            tot_r = cs_r[:, tk:]
            w_ref[wslot, h, tk:, tk:] = jnp.exp2(zs_r - cs_r[:, :tk]).astype(BF16)
            w_ref[wslot, h, :tk, :tk] = jnp.exp2(zs_l[:tk] - cs_l[:tk, :tk]).astype(BF16)
            w_ref[wslot, h, tk:, :tk] = jnp.exp2(zs_l[tk:] - cs_l[tk:, :tk] - tot_r).astype(BF16)
            car_ref[qi, h, :tk] = cs_l[:tk, tk:]
            car_ref[qi, h, tk:] = tot_r + cs_l[tk:, tk:]

    def mid_full(qi, zslot, wslot):
        parts = []
        for h in range(nh):
            for b in range(2):
                parts.append(split(z_ref[zslot, h, :, pl.ds((1 - b) * tk, tk)]))
        for h in range(nh):
            car = car_ref[qi, h]
            for b in range(2):
                zs, cat = parts[2 * h + b]
                cs = jnp.dot(cat, uj, preferred_element_type=F32)
                w_ref[wslot, h, :, pl.ds((1 - b) * tk, tk)] = jnp.exp2(zs - cs[:, :tk] - car).astype(BF16)
                car = car + cs[:, tk:]
            car_ref[qi, h] = car

    def pipeline(count, qk, pv, mid):
        qk(0, 0)

        def two_sets(m, _):
            n = 2 * m
            qk(n + 1, 1)
            pv(n - 1, 1)
            mid(n, 0, 0)
            qk(jnp.minimum(n + 2, count - 1), 0)
            pv(n, 0)
            mid(n + 1, 1, 1)
            return 0

        lax.fori_loop(0, count // 2, two_sets, 0)
        pv(count - 1, 1)

    def row0(i):
        return pl.multiple_of(i * tq, tq)

    def spare(n, qi):
        return jnp.where(n < 0, nq, qi)

    pipeline(nq,
             lambda i, slot: qk_stage(row0(i), row0(i), slot),
             lambda i, slot: pv_stage(row0(jnp.maximum(i, 0)), spare(i, i), slot, True),
             mid_diag)

    def key0(n):
        return pl.multiple_of(tbl_ref[1, jnp.maximum(n, 0)], tq)

    pipeline(nsets,
             lambda n, slot: qk_stage(pl.multiple_of(tbl_ref[0, n], tq), key0(n), slot),
             lambda n, slot: pv_stage(key0(n), spare(n, tbl_ref[2, jnp.maximum(n, 0)]), slot, False),
             lambda n, zslot, wslot: mid_full(tbl_ref[2, n], zslot, wslot))

    def finish(i, _):
        t0 = row0(i)
        for h, hs in enumerate(heads):
            g = g_ref[pl.ds(t0, tq), hs].astype(F32)
            o_ref[pl.ds(t0, tq), hs] = (acc_ref[i, h] * _silu(g)).astype(o_ref.dtype)
        return 0

    lax.fori_loop(0, nq, finish, 0)


def _sb_attn(proj, *, batch, seq, n_heads, col_q, col_k, col_v, col_g, nh, tq, tk):
    wblk = nh * HEAD_DIM
    nq = seq // tq
    tbl = _sb_sets(seq, tq)
    assert n_heads % nh == 0 and seq % tq == 0 and tq == 2 * tk and tk == V7X_LANES
    assert nq % 2 == 0 and tbl.shape[1] % 2 == 0

    def spec(col):
        base = col // wblk
        return pl.BlockSpec((seq, wblk), lambda b, hg, tbl_ref: (b, base + hg))

    blk = 5 * seq * wblk * 2 + 2 * nq * nh * tq * HEAD_DIM * 4 + nh * tq * tq * (2 * 4 + 2 * 2)
    kern = functools.partial(_sb_attn_kernel, nh=nh, tq=tq, tk=tk)
    return pl.pallas_call(
        kern,
        grid_spec=pltpu.PrefetchScalarGridSpec(
            num_scalar_prefetch=1,
            grid=(batch, n_heads // nh),
            in_specs=[spec(col_q), spec(col_k), spec(col_v), spec(col_g)],
            out_specs=pl.BlockSpec((seq, wblk), lambda b, hg, tbl_ref: (b, hg)),
            scratch_shapes=[pltpu.VMEM((nq + 1, nh, tq, HEAD_DIM), F32), pltpu.VMEM((nq, nh, tq, HEAD_DIM), F32),
                            pltpu.VMEM((2, nh, tq, tq), F32), pltpu.VMEM((2, nh, tq, tq), BF16),
                            pltpu.VMEM((tk, tk), F32)]),
        out_shape=jax.ShapeDtypeStruct((batch * seq, n_heads * HEAD_DIM), BF16),
        compiler_params=pltpu.CompilerParams(
            dimension_semantics=("parallel", "parallel"), vmem_limit_bytes=_vmem_limit(blk)),
        name="sb_attn",
    )(jnp.asarray(tbl), proj, proj, proj, proj)


_CH_TQ = 2 * CHUNK
_CH_WIN = (LEFT_CHUNKS + 2) * CHUNK
_CH_PAD = LEFT_CHUNKS * CHUNK
_CH_BASE = _CH_WIN + _CH_TQ


def _rel_base_index():
    u = np.arange(_CH_BASE)
    u = np.where(u >= _CH_WIN, u - _CH_BASE, u)
    return (np.clip(_CH_PAD - u, -(CHUNK - 1), REL_CLIP) + (CHUNK - 1)).astype(np.int32)


def _ch_attn_kernel(q_ref, k_ref, v_ref, g_ref, qg_ref, kg_ref, base_ref, o_ref,
                    kn_ref, bias_ref, *, scale):
    seq = q_ref.shape[0]
    tq, win = _CH_TQ, _CH_WIN
    nclip = _CH_PAD // tq
    nblk = seq // tq

    def rms(x, gain):
        ms = jnp.mean(x * x, axis=-1, keepdims=True)
        return x * lax.rsqrt(ms + NORM_EPS) * gain

    kn_ref[...] = rms(k_ref[...].astype(F32), kg_ref[...]).astype(BF16)
    q_gain = qg_ref[...] * (scale * LOG2_E)
    base = jnp.broadcast_to(base_ref[...] * LOG2_E, (tq, _CH_BASE))
    rolled = pltpu.roll(base, 0, 1, stride=1, stride_axis=0)[:, :win]
    qc = lax.broadcasted_iota(jnp.int32, (tq, win), 0) // CHUNK
    kc = lax.broadcasted_iota(jnp.int32, (tq, win), 1) // CHUNK
    bias_ref[...] = jnp.where((kc >= qc) & (kc <= qc + LEFT_CHUNKS), rolled, NEG_BIG)

    def blocks(t0s, k0s, widths, offs):
        def scores(n):
            qn = rms(q_ref[pl.ds(t0s[n], tq), :].astype(F32), q_gain).astype(BF16)
            kw = kn_ref[pl.ds(k0s[n], widths[n]), :]
            s = lax.dot_general(qn, kw, (((1,), (1,)), ((), ())), preferred_element_type=F32)
            return s + bias_ref[:, offs[n]:offs[n] + widths[n]]

        def softmax(s):
            p = jnp.exp2(s - jnp.max(s, axis=-1, keepdims=True))
            return p.astype(BF16), 1.0 / jnp.sum(p, axis=-1, keepdims=True)

        def output(n, p, inv_l):
            y = jnp.dot(p, v_ref[pl.ds(k0s[n], widths[n]), :], preferred_element_type=F32) * inv_l
            g = g_ref[pl.ds(t0s[n], tq), :].astype(F32)
            o_ref[pl.ds(t0s[n], tq), :] = (y * _silu(g)).astype(o_ref.dtype)

        ss = [scores(n) for n in range(len(t0s))]
        ps = [softmax(s) for s in ss]
        for n, (p, inv_l) in enumerate(ps):
            output(n, p, inv_l)

    for idx in (range(nclip), range(nclip, nblk)):
        blocks([i * tq for i in idx],
               [max(i - nclip, 0) * tq for i in idx],
               [min(i + 1, nclip + 1) * tq for i in idx],
               [max(nclip - i, 0) * tq for i in idx])


def _ch_attn(proj, q_gain, k_gain, rel_base, layer, *, batch, seq, n_heads, col_q, col_k, col_v, col_g):
    d = HEAD_DIM

    def spec(col):
        base = col // d
        return pl.BlockSpec((seq, d), lambda b, h: (b, base + h))

    assert _CH_PAD % _CH_TQ == 0 and seq % _CH_TQ == 0
    blk = 5 * seq * d * 2 + seq * d * 2 + _CH_TQ * _CH_WIN * 4
    kern = functools.partial(_ch_attn_kernel, scale=HEAD_DIM ** -0.5)
    return pl.pallas_call(
        kern,
        grid=(batch, n_heads),
        in_specs=[spec(col_q), spec(col_k), spec(col_v), spec(col_g),
                  pl.BlockSpec((None, 1, d), lambda b, h: (layer, 0, 0)),
                  pl.BlockSpec((None, 1, d), lambda b, h: (layer, 0, 0)),
                  pl.BlockSpec((None, None, 1, _CH_BASE), lambda b, h: (layer, h, 0, 0))],
        out_specs=pl.BlockSpec((seq, d), lambda b, h: (b, h)),
        out_shape=jax.ShapeDtypeStruct((batch * seq, n_heads * d), BF16),
        scratch_shapes=[pltpu.VMEM((seq, d), BF16), pltpu.VMEM((_CH_TQ, _CH_WIN), F32)],
        compiler_params=pltpu.CompilerParams(
            dimension_semantics=("parallel", "parallel"), vmem_limit_bytes=_vmem_limit(blk)),
        name="ch_attn",
    )(proj, proj, proj, proj, q_gain, k_gain, rel_base)


def kernel(x, norm_g, w_in, q_norm_g, k_norm_g, rel_bias, w_out):
    batch, seq, d_model = x.shape
    depth = w_in.shape[0]
    d_mix = w_out.shape[1]
    d_sb = d_mix // 2
    d_ch = d_mix - d_sb
    assert w_in.shape[2] == 4 * d_sb + 4 * d_ch and rel_bias.shape[2] == REL_CLIP + CHUNK
    assert seq % _CH_TQ == 0

    rel_base = rel_bias[:, :, _rel_base_index()][:, :, None, :]
    norm_g3 = norm_g[:, None, :]
    col_scale = np.ones((1, w_in.shape[2]), np.float32)
    col_scale[:, :d_sb] = HEAD_DIM ** -0.5 * LOG2_E
    q_gain3 = q_norm_g[:, None, :]
    k_gain3 = k_norm_g[:, None, :]

    x2 = x.reshape(batch * seq, d_model)
    for layer in range(depth):
        proj = _in_proj(x2, norm_g3, w_in, jnp.asarray(col_scale), layer, tm=1024, tn=1024)
        ma = _sb_attn(proj, batch=batch, seq=seq, n_heads=d_sb // HEAD_DIM,
                      col_q=0, col_k=d_sb, col_v=2 * d_sb, col_g=3 * d_sb, nh=4, tq=256, tk=128)
        c0 = 4 * d_sb
        mb = _ch_attn(proj, q_gain3, k_gain3, rel_base, layer,
                      batch=batch, seq=seq, n_heads=d_ch // HEAD_DIM,
                      col_q=c0, col_k=c0 + d_ch, col_v=c0 + 2 * d_ch, col_g=c0 + 3 * d_ch)
        x2 = _out_proj(x2, ma, mb, w_out, layer, tm=1024, tn=1024)
    return x2.reshape(batch, seq, d_model)
```

```python
import functools

import jax
import jax.numpy as jnp
import numpy as np
from jax import lax
from jax.experimental import pallas as pl
from jax.experimental.pallas import tpu as pltpu

HEAD_DIM = 128
CHUNK = 64
LEFT_CHUNKS = 8
REL_CLIP = 256
NORM_EPS = 1e-6
NEG_BIG = -1e30
LOG2_E = 1.4426950408889634

V7X_LANES = 128
V7X_VMEM_BYTES = 64 * 1024 * 1024

F32 = jnp.float32
BF16 = jnp.bfloat16


def _vmem_limit(block_bytes):
    return int(min(2 * block_bytes + 24 * 1024 * 1024, V7X_VMEM_BYTES - 8 * 1024 * 1024))


def _in_proj_kernel(x_ref, g_ref, w_ref, cs_ref, o_ref, h_ref):
    @pl.when(pl.program_id(1) == 0)
    def _():
        x = x_ref[...]
        ms = jnp.mean(x * x, axis=-1, keepdims=True)
        h_ref[...] = (x * lax.rsqrt(ms + NORM_EPS) * g_ref[...]).astype(BF16)

    w = w_ref[...].astype(BF16)
    o_ref[...] = (jnp.dot(h_ref[...], w, preferred_element_type=F32) * cs_ref[...]).astype(o_ref.dtype)


def _in_proj(x2, g, w_in, col_scale, layer, *, tm, tn):
    m, d = x2.shape
    n = w_in.shape[2]
    blk = tm * d * 4 + d * tn * 4 + tm * tn * 2 + tm * d * 2
    return pl.pallas_call(
        _in_proj_kernel,
        grid=(m // tm, n // tn),
        in_specs=[
            pl.BlockSpec((tm, d), lambda i, j: (i, 0)),
            pl.BlockSpec((None, 1, d), lambda i, j: (layer, 0, 0)),
            pl.BlockSpec((None, d, tn), lambda i, j: (layer, 0, j)),
            pl.BlockSpec((1, tn), lambda i, j: (0, j)),
        ],
        out_specs=pl.BlockSpec((tm, tn), lambda i, j: (i, j)),
        out_shape=jax.ShapeDtypeStruct((m, n), BF16),
        scratch_shapes=[pltpu.VMEM((tm, d), BF16)],
        compiler_params=pltpu.CompilerParams(
            dimension_semantics=("parallel", "arbitrary"), vmem_limit_bytes=_vmem_limit(blk)),
        name="in_proj",
    )(x2, g, w_in, col_scale)


def _out_proj_kernel(x_ref, ma_ref, mb_ref, w_ref, o_ref, wb_ref):
    @pl.when(pl.program_id(1) == 0)
    def _():
        wb_ref[...] = w_ref[...].astype(BF16)

    da = ma_ref.shape[1]
    acc = jnp.dot(ma_ref[...], wb_ref[:da, :], preferred_element_type=F32)
    acc = acc + jnp.dot(mb_ref[...], wb_ref[da:, :], preferred_element_type=F32)
    o_ref[...] = x_ref[...] + acc


def _out_proj(x2, ma, mb, w_out, layer, *, tm, tn):
    m, d = x2.shape
    da, db = ma.shape[1], mb.shape[1]
    blk = tm * tn * 4 * 2 + tm * (da + db) * 2 + (da + db) * tn * 4 + (da + db) * tn * 2
    return pl.pallas_call(
        _out_proj_kernel,
        grid=(d // tn, m // tm),
        in_specs=[
            pl.BlockSpec((tm, tn), lambda j, i: (i, j)),
            pl.BlockSpec((tm, da), lambda j, i: (i, 0)),
            pl.BlockSpec((tm, db), lambda j, i: (i, 0)),
            pl.BlockSpec((None, da + db, tn), lambda j, i: (layer, 0, j)),
        ],
        out_specs=pl.BlockSpec((tm, tn), lambda j, i: (i, j)),
        out_shape=jax.ShapeDtypeStruct((m, d), F32),
        scratch_shapes=[pltpu.VMEM((da + db, tn), BF16)],
        compiler_params=pltpu.CompilerParams(
            dimension_semantics=("parallel", "arbitrary"), vmem_limit_bytes=_vmem_limit(blk)),
        name="out_proj",
    )(x2, ma, mb, w_out)


def _silu(g):
    return g * (1.0 / (1.0 + jnp.exp(-g)))


def _sb_sets(seq, tq):
    rows = [(i * tq, (i - jj) * tq, i) for i in range(seq // tq) for jj in range(1, i + 1)]
    return np.asarray(rows, np.int32).T.copy()


def _sb_attn_kernel(tbl_ref, q_ref, k_ref, v_ref, g_ref, o_ref,
                    acc_ref, car_ref, z_ref, w_ref, bm_ref, *, nh, tq, tk):
    seq = q_ref.shape[0]
    d = HEAD_DIM
    nq = seq // tq
    nsets = tbl_ref.shape[1]

    row = lax.broadcasted_iota(jnp.int32, (2 * tk, 2 * tk), 0) % tk
    col = lax.broadcasted_iota(jnp.int32, (2 * tk, 2 * tk), 1)
    uj = jnp.where((col >= tk) | (row > col), 1.0, 0.0).astype(BF16)
    t_idx = lax.broadcasted_iota(jnp.int32, (tk, tk), 0)
    s_idx = lax.broadcasted_iota(jnp.int32, (tk, tk), 1)
    bm_ref[...] = jnp.where(s_idx < t_idx, 0.0, NEG_BIG)
    w_ref[...] = jnp.zeros_like(w_ref)

    heads = [pl.ds(h * d, d) for h in range(nh)]

    def split(z):
        sp = jnp.maximum(z, 0.0) + jnp.log2(1.0 + jnp.exp2(-jnp.abs(z)))
        hi = sp.astype(BF16)
        lo = (sp - hi.astype(F32)).astype(BF16)
        return z - sp, jnp.concatenate([hi, lo], axis=1)

    def qk_stage(t0, k0, slot):
        for h, hs in enumerate(heads):
            z_ref[slot, h] = lax.dot_general(q_ref[pl.ds(t0, tq), hs], k_ref[pl.ds(k0, tq), hs],
                                             (((1,), (1,)), ((), ())), preferred_element_type=F32)

    def pv_stage(k0, qi, slot, init):
        for h, hs in enumerate(heads):
            y = jnp.dot(w_ref[slot, h], v_ref[pl.ds(k0, tq), hs], preferred_element_type=F32)
            if init:
                acc_ref[qi, h] = y
            else:
                acc_ref[qi, h] += y

    def mid_diag(qi, zslot, wslot):
        parts = []
        for h in range(nh):
            zl = z_ref[zslot, h, :, :tk]
            zl = jnp.concatenate([zl[:tk] + bm_ref[...], zl[tk:]], axis=0)
            zr = z_ref[zslot, h, tk:, tk:] + bm_ref[...]
            parts.append(split(zl) + split(zr))
        for h in range(nh):
            zs_l, cat_l, zs_r, cat_r = parts[h]
            cs_r = jnp.dot(cat_r, uj, preferred_element_type=F32)
            cs_l = jnp.dot(cat_l, uj, preferred_element_type=F32)
            tot_r = cs_r[:, tk:]
            w_ref[wslot, h, tk:, tk:] = jnp.exp2(zs_r - cs_r[:, :tk]).astype(BF16)
            w_ref[wslot, h, :tk, :tk] = jnp.exp2(zs_l[:tk] - cs_l[:tk, :tk]).astype(BF16)
            w_ref[wslot, h, tk:, :tk] = jnp.exp2(zs_l[tk:] - cs_l[tk:, :tk] - tot_r).astype(BF16)
            car_ref[qi, h, :tk] = cs_l[:tk, tk:]
            car_ref[qi, h, tk:] = tot_r + cs_l[tk:, tk:]

    def mid_full(qi, zslot, wslot):
        parts = []
        for h in range(nh):
            for b in range(2):
                parts.append(split(z_ref[zslot, h, :, pl.ds((1 - b) * tk, tk)]))
        for h in range(nh):
            car = car_ref[qi, h]
            for b in range(2):
                zs, cat = parts[2 * h + b]
                cs = jnp.dot(cat, uj, preferred_element_type=F32)
                w_ref[wslot, h, :, pl.ds((1 - b) * tk, tk)] = jnp.exp2(zs - cs[:, :tk] - car).astype(BF16)
                car = car + cs[:, tk:]
            car_ref[qi, h] = car

    def pipeline(count, qk, pv, mid):
        qk(0, 0)

        def two_sets(m, _):
            n = 2 * m
            qk(n + 1, 1)
            pv(n - 1, 1)
            mid(n, 0, 0)
            qk(jnp.minimum(n + 2, count - 1), 0)
            pv(n, 0)
            mid(n + 1, 1, 1)
            return 0

        lax.fori_loop(0, count // 2, two_sets, 0)
        pv(count - 1, 1)

    def row0(i):
        return pl.multiple_of(i * tq, tq)

    def spare(n, qi):
        return jnp.where(n < 0, nq, qi)

    pipeline(nq,
             lambda i, slot: qk_stage(row0(i), row0(i), slot),
             lambda i, slot: pv_stage(row0(jnp.maximum(i, 0)), spare(i, i), slot, True),
             mid_diag)

    def key0(n):
        return pl.multiple_of(tbl_ref[1, jnp.maximum(n, 0)], tq)

    pipeline(nsets,
             lambda n, slot: qk_stage(pl.multiple_of(tbl_ref[0, n], tq), key0(n), slot),
             lambda n, slot: pv_stage(key0(n), spare(n, tbl_ref[2, jnp.maximum(n, 0)]), slot, False),
             lambda n, zslot, wslot: mid_full(tbl_ref[2, n], zslot, wslot))

    def finish(i, _):
        t0 = row0(i)
        for h, hs in enumerate(heads):
            g = g_ref[pl.ds(t0, tq), hs].astype(F32)
            o_ref[pl.ds(t0, tq), hs] = (acc_ref[i, h] * _silu(g)).astype(o_ref.dtype)
        return 0

    lax.fori_loop(0, nq, finish, 0)


def _sb_attn(proj, *, batch, seq, n_heads, col_q, col_k, col_v, col_g, nh, tq, tk):
    wblk = nh * HEAD_DIM
    nq = seq // tq
    tbl = _sb_sets(seq, tq)
    assert n_heads % nh == 0 and seq % tq == 0 and tq == 2 * tk and tk == V7X_LANES
    assert nq % 2 == 0 and tbl.shape[1] % 2 == 0

    def spec(col):
        base = col // wblk
        return pl.BlockSpec((seq, wblk), lambda b, hg, tbl_ref: (b, base + hg))

    blk = 5 * seq * wblk * 2 + 2 * nq * nh * tq * HEAD_DIM * 4 + nh * tq * tq * (2 * 4 + 2 * 2)
    kern = functools.partial(_sb_attn_kernel, nh=nh, tq=tq, tk=tk)
    return pl.pallas_call(
        kern,
        grid_spec=pltpu.PrefetchScalarGridSpec(
            num_scalar_prefetch=1,
            grid=(batch, n_heads // nh),
            in_specs=[spec(col_q), spec(col_k), spec(col_v), spec(col_g)],
            out_specs=pl.BlockSpec((seq, wblk), lambda b, hg, tbl_ref: (b, hg)),
            scratch_shapes=[pltpu.VMEM((nq + 1, nh, tq, HEAD_DIM), F32), pltpu.VMEM((nq, nh, tq, HEAD_DIM), F32),
                            pltpu.VMEM((2, nh, tq, tq), F32), pltpu.VMEM((2, nh, tq, tq), BF16),
                            pltpu.VMEM((tk, tk), F32)]),
        out_shape=jax.ShapeDtypeStruct((batch * seq, n_heads * HEAD_DIM), BF16),
        compiler_params=pltpu.CompilerParams(
            dimension_semantics=("parallel", "parallel"), vmem_limit_bytes=_vmem_limit(blk)),
        name="sb_attn",
    )(jnp.asarray(tbl), proj, proj, proj, proj)


_CH_TQ = 2 * CHUNK
_CH_WIN = (LEFT_CHUNKS + 2) * CHUNK
_CH_PAD = LEFT_CHUNKS * CHUNK
_CH_BASE = _CH_WIN + _CH_TQ


def _rel_base_index():
    u = np.arange(_CH_BASE)
    u = np.where(u >= _CH_WIN, u - _CH_BASE, u)
    return (np.clip(_CH_PAD - u, -(CHUNK - 1), REL_CLIP) + (CHUNK - 1)).astype(np.int32)


def _ch_attn_kernel(q_ref, k_ref, v_ref, g_ref, qg_ref, kg_ref, base_ref, o_ref,
                    kn_ref, bias_ref, *, scale):
    seq = q_ref.shape[0]
    tq, win = _CH_TQ, _CH_WIN
    nclip = _CH_PAD // tq
    nblk = seq // tq

    def rms(x, gain):
        ms = jnp.mean(x * x, axis=-1, keepdims=True)
        return x * lax.rsqrt(ms + NORM_EPS) * gain

    kn_ref[...] = rms(k_ref[...].astype(F32), kg_ref[...]).astype(BF16)
    q_gain = qg_ref[...] * (scale * LOG2_E)

    @pl.when(pl.program_id(1) == 0)
    def _():
        base = jnp.broadcast_to(base_ref[...] * LOG2_E, (tq, _CH_BASE))
        rolled = pltpu.roll(base, 0, 1, stride=1, stride_axis=0)[:, :win]
        qc = lax.broadcasted_iota(jnp.int32, (tq, win), 0) // CHUNK
        kc = lax.broadcasted_iota(jnp.int32, (tq, win), 1) // CHUNK
        bias_ref[...] = jnp.where((kc >= qc) & (kc <= qc + LEFT_CHUNKS), rolled, NEG_BIG)

    def blocks(t0s, k0s, widths, offs):
        def scores(n):
            qn = rms(q_ref[pl.ds(t0s[n], tq), :].astype(F32), q_gain).astype(BF16)
            kw = kn_ref[pl.ds(k0s[n], widths[n]), :]
            s = lax.dot_general(qn, kw, (((1,), (1,)), ((), ())), preferred_element_type=F32)
            return s + bias_ref[:, offs[n]:offs[n] + widths[n]]

        def softmax(s):
            p = jnp.exp2(s - jnp.max(s, axis=-1, keepdims=True))
            return p.astype(BF16), 1.0 / jnp.sum(p, axis=-1, keepdims=True)

        def output(n, p, inv_l):
            y = jnp.dot(p, v_ref[pl.ds(k0s[n], widths[n]), :], preferred_element_type=F32) * inv_l
            g = g_ref[pl.ds(t0s[n], tq), :].astype(F32)
            o_ref[pl.ds(t0s[n], tq), :] = (y * _silu(g)).astype(o_ref.dtype)

        ss = [scores(n) for n in range(len(t0s))]
        ps = [softmax(s) for s in ss]
        for n, (p, inv_l) in enumerate(ps):
            output(n, p, inv_l)

    for idx in (range(nclip), range(nclip, nblk)):
        blocks([i * tq for i in idx],
               [max(i - nclip, 0) * tq for i in idx],
               [min(i + 1, nclip + 1) * tq for i in idx],
               [max(nclip - i, 0) * tq for i in idx])


def _ch_attn(proj, q_gain, k_gain, rel_base, layer, *, batch, seq, n_heads, col_q, col_k, col_v, col_g):
    d = HEAD_DIM

    def spec(col):
        base = col // d
        return pl.BlockSpec((seq, d), lambda h, b: (b, base + h))

    assert _CH_PAD % _CH_TQ == 0 and seq % _CH_TQ == 0
    blk = 5 * seq * d * 2 + seq * d * 2 + _CH_TQ * _CH_WIN * 4
    kern = functools.partial(_ch_attn_kernel, scale=HEAD_DIM ** -0.5)
    return pl.pallas_call(
        kern,
        grid=(n_heads, batch),
        in_specs=[spec(col_q), spec(col_k), spec(col_v), spec(col_g),
                  pl.BlockSpec((None, 1, d), lambda h, b: (layer, 0, 0)),
                  pl.BlockSpec((None, 1, d), lambda h, b: (layer, 0, 0)),
                  pl.BlockSpec((None, None, 1, _CH_BASE), lambda h, b: (layer, h, 0, 0))],
        out_specs=pl.BlockSpec((seq, d), lambda h, b: (b, h)),
        out_shape=jax.ShapeDtypeStruct((batch * seq, n_heads * d), BF16),
        scratch_shapes=[pltpu.VMEM((seq, d), BF16), pltpu.VMEM((_CH_TQ, _CH_WIN), F32)],
        compiler_params=pltpu.CompilerParams(
            dimension_semantics=("parallel", "arbitrary"), vmem_limit_bytes=_vmem_limit(blk)),
        name="ch_attn",
    )(proj, proj, proj, proj, q_gain, k_gain, rel_base)


def kernel(x, norm_g, w_in, q_norm_g, k_norm_g, rel_bias, w_out):
    batch, seq, d_model = x.shape
    depth = w_in.shape[0]
    d_mix = w_out.shape[1]
    d_sb = d_mix // 2
    d_ch = d_mix - d_sb
    assert w_in.shape[2] == 4 * d_sb + 4 * d_ch and rel_bias.shape[2] == REL_CLIP + CHUNK
    assert seq % _CH_TQ == 0

    rel_base = rel_bias[:, :, _rel_base_index()][:, :, None, :]
    norm_g3 = norm_g[:, None, :]
    col_scale = np.ones((1, w_in.shape[2]), np.float32)
    col_scale[:, :d_sb] = HEAD_DIM ** -0.5 * LOG2_E
    q_gain3 = q_norm_g[:, None, :]
    k_gain3 = k_norm_g[:, None, :]

    x2 = x.reshape(batch * seq, d_model)
    for layer in range(depth):
        proj = _in_proj(x2, norm_g3, w_in, jnp.asarray(col_scale), layer, tm=1024, tn=1024)
        ma = _sb_attn(proj, batch=batch, seq=seq, n_heads=d_sb // HEAD_DIM,
                      col_q=0, col_k=d_sb, col_v=2 * d_sb, col_g=3 * d_sb, nh=4, tq=256, tk=128)
        c0 = 4 * d_sb
        mb = _ch_attn(proj, q_gain3, k_gain3, rel_base, layer,
                      batch=batch, seq=seq, n_heads=d_ch // HEAD_DIM,
                      col_q=c0, col_k=c0 + d_ch, col_v=c0 + 2 * d_ch, col_g=c0 + 3 * d_ch)
        x2 = _out_proj(x2, ma, mb, w_out, layer, tm=1024, tn=1024)
    return x2.reshape(batch, seq, d_model)
```

```python
import functools

import jax
import jax.numpy as jnp
import numpy as np
from jax import lax
from jax.experimental import pallas as pl
from jax.experimental.pallas import tpu as pltpu

HEAD_DIM = 128
CHUNK = 64
LEFT_CHUNKS = 8
REL_CLIP = 256
NORM_EPS = 1e-6
NEG_BIG = -1e30
LOG2_E = 1.4426950408889634

V7X_LANES = 128
V7X_VMEM_BYTES = 64 * 1024 * 1024

F32 = jnp.float32
BF16 = jnp.bfloat16


def _vmem_limit(block_bytes):
    return int(min(2 * block_bytes + 24 * 1024 * 1024, V7X_VMEM_BYTES - 8 * 1024 * 1024))


def _in_proj_kernel(x_ref, g_ref, w_ref, cs_ref, o_ref, h_ref):
    @pl.when(pl.program_id(1) == 0)
    def _():
        x = x_ref[...]
        ms = jnp.mean(x * x, axis=-1, keepdims=True)
        h_ref[...] = (x * lax.rsqrt(ms + NORM_EPS) * g_ref[...]).astype(BF16)

    w = w_ref[...].astype(BF16)
    o_ref[...] = (jnp.dot(h_ref[...], w, preferred_element_type=F32) * cs_ref[...]).astype(o_ref.dtype)


def _in_proj(x2, g, w_in, col_scale, layer, *, tm, tn):
    m, d = x2.shape
    n = w_in.shape[2]
    blk = tm * d * 4 + d * tn * 4 + tm * tn * 2 + tm * d * 2
    return pl.pallas_call(
        _in_proj_kernel,
        grid=(m // tm, n // tn),
        in_specs=[
            pl.BlockSpec((tm, d), lambda i, j: (i, 0)),
            pl.BlockSpec((None, 1, d), lambda i, j: (layer, 0, 0)),
            pl.BlockSpec((None, d, tn), lambda i, j: (layer, 0, j)),
            pl.BlockSpec((1, tn), lambda i, j: (0, j)),
        ],
        out_specs=pl.BlockSpec((tm, tn), lambda i, j: (i, j)),
        out_shape=jax.ShapeDtypeStruct((m, n), BF16),
        scratch_shapes=[pltpu.VMEM((tm, d), BF16)],
        compiler_params=pltpu.CompilerParams(
            dimension_semantics=("parallel", "arbitrary"), vmem_limit_bytes=_vmem_limit(blk)),
        name="in_proj",
    )(x2, g, w_in, col_scale)


def _out_proj_kernel(x_ref, ma_ref, mb_ref, w_ref, o_ref, wb_ref):
    @pl.when(pl.program_id(1) == 0)
    def _():
        wb_ref[...] = w_ref[...].astype(BF16)

    da = ma_ref.shape[1]
    acc = jnp.dot(ma_ref[...], wb_ref[:da, :], preferred_element_type=F32)
    acc = acc + jnp.dot(mb_ref[...], wb_ref[da:, :], preferred_element_type=F32)
    o_ref[...] = x_ref[...] + acc


def _out_proj(x2, ma, mb, w_out, layer, *, tm, tn):
    m, d = x2.shape
    da, db = ma.shape[1], mb.shape[1]
    blk = tm * tn * 4 * 2 + tm * (da + db) * 2 + (da + db) * tn * 4 + (da + db) * tn * 2
    return pl.pallas_call(
        _out_proj_kernel,
        grid=(d // tn, m // tm),
        in_specs=[
            pl.BlockSpec((tm, tn), lambda j, i: (i, j)),
            pl.BlockSpec((tm, da), lambda j, i: (i, 0)),
            pl.BlockSpec((tm, db), lambda j, i: (i, 0)),
            pl.BlockSpec((None, da + db, tn), lambda j, i: (layer, 0, j), pipeline_mode=pl.Buffered(1)),
        ],
        out_specs=pl.BlockSpec((tm, tn), lambda j, i: (i, j)),
        out_shape=jax.ShapeDtypeStruct((m, d), F32),
        scratch_shapes=[pltpu.VMEM((da + db, tn), BF16)],
        compiler_params=pltpu.CompilerParams(
            dimension_semantics=("parallel", "arbitrary"), vmem_limit_bytes=_vmem_limit(blk)),
        name="out_proj",
    )(x2, ma, mb, w_out)


def _silu(g):
    return g * (1.0 / (1.0 + jnp.exp(-g)))


def _sb_sets(seq, tq):
    rows = [(i * tq, (i - jj) * tq, i) for i in range(seq // tq) for jj in range(1, i + 1)]
    return np.asarray(rows, np.int32).T.copy()


def _sb_attn_kernel(tbl_ref, q_ref, k_ref, v_ref, g_ref, o_ref,
                    acc_ref, car_ref, z_ref, w_ref, bm_ref, *, nh, tq, tk):
    seq = q_ref.shape[0]
    d = HEAD_DIM
    nq = seq // tq
    nsets = tbl_ref.shape[1]

    row = lax.broadcasted_iota(jnp.int32, (2 * tk, 2 * tk), 0) % tk
    col = lax.broadcasted_iota(jnp.int32, (2 * tk, 2 * tk), 1)
    uj = jnp.where((col >= tk) | (row > col), 1.0, 0.0).astype(BF16)
    t_idx = lax.broadcasted_iota(jnp.int32, (tk, tk), 0)
    s_idx = lax.broadcasted_iota(jnp.int32, (tk, tk), 1)
    bm_ref[...] = jnp.where(s_idx < t_idx, 0.0, NEG_BIG)
    w_ref[...] = jnp.zeros_like(w_ref)

    heads = [pl.ds(h * d, d) for h in range(nh)]

    def split(z):
        sp = jnp.maximum(z, 0.0) + jnp.log2(1.0 + jnp.exp2(-jnp.abs(z)))
        hi = sp.astype(BF16)
        lo = (sp - hi.astype(F32)).astype(BF16)
        return z - sp, jnp.concatenate([hi, lo], axis=1)

    def qk_stage(t0, k0, slot):
        for h, hs in enumerate(heads):
            z_ref[slot, h] = lax.dot_general(q_ref[pl.ds(t0, tq), hs], k_ref[pl.ds(k0, tq), hs],
                                             (((1,), (1,)), ((), ())), preferred_element_type=F32)

    def pv_stage(k0, qi, slot, init):
        for h, hs in enumerate(heads):
            y = jnp.dot(w_ref[slot, h], v_ref[pl.ds(k0, tq), hs], preferred_element_type=F32)
            if init:
                acc_ref[qi, h] = y
            else:
                acc_ref[qi, h] += y

    def mid_diag(qi, zslot, wslot):
        parts = []
        for h in range(nh):
            zl = z_ref[zslot, h, :, :tk]
            zl = jnp.concatenate([zl[:tk] + bm_ref[...], zl[tk:]], axis=0)
            zr = z_ref[zslot, h, tk:, tk:] + bm_ref[...]
            parts.append(split(zl) + split(zr))
        for h in range(nh):
            zs_l, cat_l, zs_r, cat_r = parts[h]
            cs_r = jnp.dot(cat_r, uj, preferred_element_type=F32)
            cs_l = jnp.dot(cat_l, uj, preferred_element_type=F32)
            tot_r = cs_r[:, tk:]
            w_ref[wslot, h, tk:, tk:] = jnp.exp2(zs_r - cs_r[:, :tk]).astype(BF16)
            w_ref[wslot, h, :tk, :tk] = jnp.exp2(zs_l[:tk] - cs_l[:tk, :tk]).astype(BF16)
            w_ref[wslot, h, tk:, :tk] = jnp.exp2(zs_l[tk:] - cs_l[tk:, :tk] - tot_r).astype(BF16)
            car_ref[qi, h, :tk] = cs_l[:tk, tk:]
            car_ref[qi, h, tk:] = tot_r + cs_l[tk:, tk:]

    def mid_full(qi, zslot, wslot):
        parts = []
        for h in range(nh):
            for b in range(2):
                parts.append(split(z_ref[zslot, h, :, pl.ds((1 - b) * tk, tk)]))
        for h in range(nh):
            car = car_ref[qi, h]
            for b in range(2):
                zs, cat = parts[2 * h + b]
                cs = jnp.dot(cat, uj, preferred_element_type=F32)
                w_ref[wslot, h, :, pl.ds((1 - b) * tk, tk)] = jnp.exp2(zs - cs[:, :tk] - car).astype(BF16)
                car = car + cs[:, tk:]
            car_ref[qi, h] = car

    def pipeline(count, qk, pv, mid):
        qk(0, 0)

        def two_sets(m, _):
            n = 2 * m
            qk(n + 1, 1)
            pv(n - 1, 1)
            mid(n, 0, 0)
            qk(jnp.minimum(n + 2, count - 1), 0)
            pv(n, 0)
            mid(n + 1, 1, 1)
            return 0

        lax.fori_loop(0, count // 2, two_sets, 0)
        pv(count - 1, 1)

    def row0(i):
        return pl.multiple_of(i * tq, tq)

    def spare(n, qi):
        return jnp.where(n < 0, nq, qi)

    pipeline(nq,
             lambda i, slot: qk_stage(row0(i), row0(i), slot),
             lambda i, slot: pv_stage(row0(jnp.maximum(i, 0)), spare(i, i), slot, True),
             mid_diag)

    def key0(n):
        return pl.multiple_of(tbl_ref[1, jnp.maximum(n, 0)], tq)

    pipeline(nsets,
             lambda n, slot: qk_stage(pl.multiple_of(tbl_ref[0, n], tq), key0(n), slot),
             lambda n, slot: pv_stage(key0(n), spare(n, tbl_ref[2, jnp.maximum(n, 0)]), slot, False),
             lambda n, zslot, wslot: mid_full(tbl_ref[2, n], zslot, wslot))

    def finish(i, _):
        t0 = row0(i)
        for h, hs in enumerate(heads):
            g = g_ref[pl.ds(t0, tq), hs].astype(F32)
            o_ref[pl.ds(t0, tq), hs] = (acc_ref[i, h] * _silu(g)).astype(o_ref.dtype)
        return 0

    lax.fori_loop(0, nq, finish, 0)


def _sb_attn(proj, *, batch, seq, n_heads, col_q, col_k, col_v, col_g, nh, tq, tk):
    wblk = nh * HEAD_DIM
    nq = seq // tq
    tbl = _sb_sets(seq, tq)
    assert n_heads % nh == 0 and seq % tq == 0 and tq == 2 * tk and tk == V7X_LANES
    assert nq % 2 == 0 and tbl.shape[1] % 2 == 0

    def spec(col):
        base = col // wblk
        return pl.BlockSpec((seq, wblk), lambda b, hg, tbl_ref: (b, base + hg))

    blk = 5 * seq * wblk * 2 + 2 * nq * nh * tq * HEAD_DIM * 4 + nh * tq * tq * (2 * 4 + 2 * 2)
    kern = functools.partial(_sb_attn_kernel, nh=nh, tq=tq, tk=tk)
    return pl.pallas_call(
        kern,
        grid_spec=pltpu.PrefetchScalarGridSpec(
            num_scalar_prefetch=1,
            grid=(batch, n_heads // nh),
            in_specs=[spec(col_q), spec(col_k), spec(col_v), spec(col_g)],
            out_specs=pl.BlockSpec((seq, wblk), lambda b, hg, tbl_ref: (b, hg)),
            scratch_shapes=[pltpu.VMEM((nq + 1, nh, tq, HEAD_DIM), F32), pltpu.VMEM((nq, nh, tq, HEAD_DIM), F32),
                            pltpu.VMEM((2, nh, tq, tq), F32), pltpu.VMEM((2, nh, tq, tq), BF16),
                            pltpu.VMEM((tk, tk), F32)]),
        out_shape=jax.ShapeDtypeStruct((batch * seq, n_heads * HEAD_DIM), BF16),
        compiler_params=pltpu.CompilerParams(
            dimension_semantics=("parallel", "parallel"), vmem_limit_bytes=_vmem_limit(blk)),
        name="sb_attn",
    )(jnp.asarray(tbl), proj, proj, proj, proj)


_CH_TQ = 2 * CHUNK
_CH_WIN = (LEFT_CHUNKS + 2) * CHUNK
_CH_PAD = LEFT_CHUNKS * CHUNK
_CH_BASE = _CH_WIN + _CH_TQ


def _rel_base_index():
    u = np.arange(_CH_BASE)
    u = np.where(u >= _CH_WIN, u - _CH_BASE, u)
    return (np.clip(_CH_PAD - u, -(CHUNK - 1), REL_CLIP) + (CHUNK - 1)).astype(np.int32)


def _ch_attn_kernel(q_ref, k_ref, v_ref, g_ref, qg_ref, kg_ref, base_ref, o_ref,
                    kn_ref, bias_ref, *, scale):
    seq = q_ref.shape[0]
    tq, win = _CH_TQ, _CH_WIN
    nclip = _CH_PAD // tq
    nblk = seq // tq

    def rms(x, gain):
        ms = jnp.mean(x * x, axis=-1, keepdims=True)
        return x * lax.rsqrt(ms + NORM_EPS) * gain

    kn_ref[...] = rms(k_ref[...].astype(F32), kg_ref[...]).astype(BF16)
    q_gain = qg_ref[...] * (scale * LOG2_E)

    @pl.when(pl.program_id(1) == 0)
    def _():
        base = jnp.broadcast_to(base_ref[...] * LOG2_E, (tq, _CH_BASE))
        rolled = pltpu.roll(base, 0, 1, stride=1, stride_axis=0)[:, :win]
        qc = lax.broadcasted_iota(jnp.int32, (tq, win), 0) // CHUNK
        kc = lax.broadcasted_iota(jnp.int32, (tq, win), 1) // CHUNK
        bias_ref[...] = jnp.where((kc >= qc) & (kc <= qc + LEFT_CHUNKS), rolled, NEG_BIG)

    def blocks(t0s, k0s, widths, offs):
        def scores(n):
            qn = rms(q_ref[pl.ds(t0s[n], tq), :].astype(F32), q_gain).astype(BF16)
            kw = kn_ref[pl.ds(k0s[n], widths[n]), :]
            s = lax.dot_general(qn, kw, (((1,), (1,)), ((), ())), preferred_element_type=F32)
            return s + bias_ref[:, offs[n]:offs[n] + widths[n]]

        def softmax(s):
            p = jnp.exp2(s - jnp.max(s, axis=-1, keepdims=True))
            return p.astype(BF16), 1.0 / jnp.sum(p, axis=-1, keepdims=True)

        def output(n, p, inv_l):
            y = jnp.dot(p, v_ref[pl.ds(k0s[n], widths[n]), :], preferred_element_type=F32) * inv_l
            g = g_ref[pl.ds(t0s[n], tq), :].astype(F32)
            o_ref[pl.ds(t0s[n], tq), :] = (y * _silu(g)).astype(o_ref.dtype)

        ss = [scores(n) for n in range(len(t0s))]
        ps = [softmax(s) for s in ss]
        for n, (p, inv_l) in enumerate(ps):
            output(n, p, inv_l)

    for idx in (range(nclip), range(nclip, nblk)):
        blocks([i * tq for i in idx],
               [max(i - nclip, 0) * tq for i in idx],
               [min(i + 1, nclip + 1) * tq for i in idx],
               [max(nclip - i, 0) * tq for i in idx])


def _ch_attn(proj, q_gain, k_gain, rel_base, layer, *, batch, seq, n_heads, col_q, col_k, col_v, col_g):
    d = HEAD_DIM

    def spec(col):
        base = col // d
        return pl.BlockSpec((seq, d), lambda h, b: (b, base + h))

    assert _CH_PAD % _CH_TQ == 0 and seq % _CH_TQ == 0
    blk = 5 * seq * d * 2 + seq * d * 2 + _CH_TQ * _CH_WIN * 4
    kern = functools.partial(_ch_attn_kernel, scale=HEAD_DIM ** -0.5)
    return pl.pallas_call(
        kern,
        grid=(n_heads, batch),
        in_specs=[spec(col_q), spec(col_k), spec(col_v), spec(col_g),
                  pl.BlockSpec((None, 1, d), lambda h, b: (layer, 0, 0)),
                  pl.BlockSpec((None, 1, d), lambda h, b: (layer, 0, 0)),
                  pl.BlockSpec((None, None, 1, _CH_BASE), lambda h, b: (layer, h, 0, 0))],
        out_specs=pl.BlockSpec((seq, d), lambda h, b: (b, h)),
        out_shape=jax.ShapeDtypeStruct((batch * seq, n_heads * d), BF16),
        scratch_shapes=[pltpu.VMEM((seq, d), BF16), pltpu.VMEM((_CH_TQ, _CH_WIN), F32)],
        compiler_params=pltpu.CompilerParams(
            dimension_semantics=("parallel", "arbitrary"), vmem_limit_bytes=_vmem_limit(blk)),
        name="ch_attn",
    )(proj, proj, proj, proj, q_gain, k_gain, rel_base)


def kernel(x, norm_g, w_in, q_norm_g, k_norm_g, rel_bias, w_out):
    batch, seq, d_model = x.shape
    depth = w_in.shape[0]
    d_mix = w_out.shape[1]
    d_sb = d_mix // 2
    d_ch = d_mix - d_sb
    assert w_in.shape[2] == 4 * d_sb + 4 * d_ch and rel_bias.shape[2] == REL_CLIP + CHUNK
    assert seq % _CH_TQ == 0

    rel_base = rel_bias[:, :, _rel_base_index()][:, :, None, :]
    norm_g3 = norm_g[:, None, :]
    col_scale = np.ones((1, w_in.shape[2]), np.float32)
    col_scale[:, :d_sb] = HEAD_DIM ** -0.5 * LOG2_E
    q_gain3 = q_norm_g[:, None, :]
    k_gain3 = k_norm_g[:, None, :]

    x2 = x.reshape(batch * seq, d_model)
    for layer in range(depth):
        proj = _in_proj(x2, norm_g3, w_in, jnp.asarray(col_scale), layer, tm=1024, tn=1024)
        ma = _sb_attn(proj, batch=batch, seq=seq, n_heads=d_sb // HEAD_DIM,
                      col_q=0, col_k=d_sb, col_v=2 * d_sb, col_g=3 * d_sb, nh=4, tq=256, tk=128)
        c0 = 4 * d_sb
        mb = _ch_attn(proj, q_gain3, k_gain3, rel_base, layer,
                      batch=batch, seq=seq, n_heads=d_ch // HEAD_DIM,
                      col_q=c0, col_k=c0 + d_ch, col_v=c0 + 2 * d_ch, col_g=c0 + 3 * d_ch)
        x2 = _out_proj(x2, ma, mb, w_out, layer, tm=512, tn=d_model)
    return x2.reshape(batch, seq, d_model)
```

```python
import functools

import jax
import jax.numpy as jnp
import numpy as np
from jax import lax
from jax.experimental import pallas as pl
from jax.experimental.pallas import tpu as pltpu

HEAD_DIM = 128
CHUNK = 64
LEFT_CHUNKS = 8
REL_CLIP = 256
NORM_EPS = 1e-6
NEG_BIG = -1e30
LOG2_E = 1.4426950408889634

V7X_LANES = 128
V7X_VMEM_BYTES = 64 * 1024 * 1024

F32 = jnp.float32
BF16 = jnp.bfloat16


def _vmem_limit(block_bytes):
    return int(min(2 * block_bytes + 24 * 1024 * 1024, V7X_VMEM_BYTES - 8 * 1024 * 1024))


_NORM_SLICES = 8

def _in_proj_kernel(x_ref, g_ref, w_ref, cs_ref, o_ref, h_ref):
    rows = x_ref.shape[0] // _NORM_SLICES

    @pl.when(pl.program_id(1) == 0)
    def _():
        w = w_ref[...].astype(BF16)
        for r in range(_NORM_SLICES):
            sl = pl.ds(r * rows, rows)
            x = x_ref[sl, :]
            ms = jnp.mean(x * x, axis=-1, keepdims=True)
            h = (x * lax.rsqrt(ms + NORM_EPS) * g_ref[...]).astype(BF16)
            h_ref[sl, :] = h
            o_ref[sl, :] = (jnp.dot(h, w, preferred_element_type=F32) * cs_ref[...]).astype(o_ref.dtype)

    @pl.when(pl.program_id(1) != 0)
    def _():
        w = w_ref[...].astype(BF16)
        o_ref[...] = (jnp.dot(h_ref[...], w, preferred_element_type=F32) * cs_ref[...]).astype(o_ref.dtype)


def _in_proj(x2, g, w_in, col_scale, layer, *, tm, tn):
    m, d = x2.shape
    n = w_in.shape[2]
    blk = tm * d * 4 + d * tn * 4 + tm * tn * 2 + tm * d * 2
    return pl.pallas_call(
        _in_proj_kernel,
        grid=(m // tm, n // tn),
        in_specs=[
            pl.BlockSpec((tm, d), lambda i, j: (i, 0)),
            pl.BlockSpec((None, 1, d), lambda i, j: (layer, 0, 0)),
            pl.BlockSpec((None, d, tn), lambda i, j: (layer, 0, j)),
            pl.BlockSpec((1, tn), lambda i, j: (0, j)),
        ],
        out_specs=pl.BlockSpec((tm, tn), lambda i, j: (i, j)),
        out_shape=jax.ShapeDtypeStruct((m, n), BF16),
        scratch_shapes=[pltpu.VMEM((tm, d), BF16)],
        compiler_params=pltpu.CompilerParams(
            dimension_semantics=("parallel", "arbitrary"), vmem_limit_bytes=_vmem_limit(blk)),
        name="in_proj",
    )(x2, g, w_in, col_scale)


def _out_proj_kernel(x_ref, ma_ref, mb_ref, w_ref, o_ref, wb_ref):
    @pl.when(pl.program_id(1) == 0)
    def _():
        wb_ref[...] = w_ref[...].astype(BF16)

    da = ma_ref.shape[1]
    acc = jnp.dot(ma_ref[...], wb_ref[:da, :], preferred_element_type=F32)
    acc = acc + jnp.dot(mb_ref[...], wb_ref[da:, :], preferred_element_type=F32)
    o_ref[...] = x_ref[...] + acc


def _out_proj(x2, ma, mb, w_out, layer, *, tm, tn):
    m, d = x2.shape
    da, db = ma.shape[1], mb.shape[1]
    blk = tm * tn * 4 * 2 + tm * (da + db) * 2 + (da + db) * tn * 4 + (da + db) * tn * 2
    return pl.pallas_call(
        _out_proj_kernel,
        grid=(d // tn, m // tm),
        in_specs=[
            pl.BlockSpec((tm, tn), lambda j, i: (i, j)),
            pl.BlockSpec((tm, da), lambda j, i: (i, 0)),
            pl.BlockSpec((tm, db), lambda j, i: (i, 0)),
            pl.BlockSpec((None, da + db, tn), lambda j, i: (layer, 0, j), pipeline_mode=pl.Buffered(1)),
        ],
        out_specs=pl.BlockSpec((tm, tn), lambda j, i: (i, j)),
        out_shape=jax.ShapeDtypeStruct((m, d), F32),
        scratch_shapes=[pltpu.VMEM((da + db, tn), BF16)],
        compiler_params=pltpu.CompilerParams(
            dimension_semantics=("parallel", "arbitrary"), vmem_limit_bytes=_vmem_limit(blk)),
        name="out_proj",
    )(x2, ma, mb, w_out)


def _silu(g):
    return g * (1.0 / (1.0 + jnp.exp(-g)))


def _sb_sets(seq, tq):
    rows = [(i * tq, (i - jj) * tq, i) for i in range(seq // tq) for jj in range(1, i + 1)]
    return np.asarray(rows, np.int32).T.copy()


def _sb_attn_kernel(tbl_ref, q_ref, k_ref, v_ref, g_ref, o_ref,
                    acc_ref, car_ref, z_ref, w_ref, bm_ref, *, nh, tq, tk):
    seq = q_ref.shape[0]
    d = HEAD_DIM
    nq = seq // tq
    nsets = tbl_ref.shape[1]

    row = lax.broadcasted_iota(jnp.int32, (2 * tk, 2 * tk), 0) % tk
    col = lax.broadcasted_iota(jnp.int32, (2 * tk, 2 * tk), 1)
    uj = jnp.where((col >= tk) | (row > col), 1.0, 0.0).astype(BF16)
    t_idx = lax.broadcasted_iota(jnp.int32, (tk, tk), 0)
    s_idx = lax.broadcasted_iota(jnp.int32, (tk, tk), 1)
    bm_ref[...] = jnp.where(s_idx < t_idx, 0.0, NEG_BIG)
    w_ref[...] = jnp.zeros_like(w_ref)

    heads = [pl.ds(h * d, d) for h in range(nh)]

    def split(z):
        sp = jnp.maximum(z, 0.0) + jnp.log2(1.0 + jnp.exp2(-jnp.abs(z)))
        hi = sp.astype(BF16)
        lo = (sp - hi.astype(F32)).astype(BF16)
        return z - sp, jnp.concatenate([hi, lo], axis=1)

    def qk_stage(t0, k0, slot):
        for h, hs in enumerate(heads):
            z_ref[slot, h] = lax.dot_general(q_ref[pl.ds(t0, tq), hs], k_ref[pl.ds(k0, tq), hs],
                                             (((1,), (1,)), ((), ())), preferred_element_type=F32)

    def pv_stage(k0, qi, slot, init):
        for h, hs in enumerate(heads):
            y = jnp.dot(w_ref[slot, h], v_ref[pl.ds(k0, tq), hs], preferred_element_type=F32)
            if init:
                acc_ref[qi, h] = y
            else:
                acc_ref[qi, h] += y

    def mid_diag(qi, zslot, wslot):
        parts = []
        for h in range(nh):
            zl = z_ref[zslot, h, :, :tk]
            zl = jnp.concatenate([zl[:tk] + bm_ref[...], zl[tk:]], axis=0)
            zr = z_ref[zslot, h, tk:, tk:] + bm_ref[...]
            parts.append(split(zl) + split(zr))
        for h in range(nh):
            zs_l, cat_l, zs_r, cat_r = parts[h]
            cs_r = jnp.dot(cat_r, uj, preferred_element_type=F32)
            cs_l = jnp.dot(cat_l, uj, preferred_element_type=F32)
            tot_r = cs_r[:, tk:]
            w_ref[wslot, h, tk:, tk:] = jnp.exp2(zs_r - cs_r[:, :tk]).astype(BF16)
            w_ref[wslot, h, :tk, :tk] = jnp.exp2(zs_l[:tk] - cs_l[:tk, :tk]).astype(BF16)
            w_ref[wslot, h, tk:, :tk] = jnp.exp2(zs_l[tk:] - cs_l[tk:, :tk] - tot_r).astype(BF16)
            car_ref[qi, h, :tk] = cs_l[:tk, tk:]
            car_ref[qi, h, tk:] = tot_r + cs_l[tk:, tk:]

    def mid_full(qi, zslot, wslot):
        parts = []
        for h in range(nh):
            for b in range(2):
                parts.append(split(z_ref[zslot, h, :, pl.ds((1 - b) * tk, tk)]))
        for h in range(nh):
            car = car_ref[qi, h]
            for b in range(2):
                zs, cat = parts[2 * h + b]
                cs = jnp.dot(cat, uj, preferred_element_type=F32)
                w_ref[wslot, h, :, pl.ds((1 - b) * tk, tk)] = jnp.exp2(zs - cs[:, :tk] - car).astype(BF16)
                car = car + cs[:, tk:]
            car_ref[qi, h] = car

    def pipeline(count, qk, pv, mid):
        qk(0, 0)

        def two_sets(m, _):
            n = 2 * m
            qk(n + 1, 1)
            pv(n - 1, 1)
            mid(n, 0, 0)
            qk(jnp.minimum(n + 2, count - 1), 0)
            pv(n, 0)
            mid(n + 1, 1, 1)
            return 0

        lax.fori_loop(0, count // 2, two_sets, 0)
        pv(count - 1, 1)

    def row0(i):
        return pl.multiple_of(i * tq, tq)

    def spare(n, qi):
        return jnp.where(n < 0, nq, qi)

    pipeline(nq,
             lambda i, slot: qk_stage(row0(i), row0(i), slot),
             lambda i, slot: pv_stage(row0(jnp.maximum(i, 0)), spare(i, i), slot, True),
             mid_diag)

    def key0(n):
        return pl.multiple_of(tbl_ref[1, jnp.maximum(n, 0)], tq)

    pipeline(nsets,
             lambda n, slot: qk_stage(pl.multiple_of(tbl_ref[0, n], tq), key0(n), slot),
             lambda n, slot: pv_stage(key0(n), spare(n, tbl_ref[2, jnp.maximum(n, 0)]), slot, False),
             lambda n, zslot, wslot: mid_full(tbl_ref[2, n], zslot, wslot))

    def finish(i, _):
        t0 = row0(i)
        for h, hs in enumerate(heads):
            g = g_ref[pl.ds(t0, tq), hs].astype(F32)
            o_ref[pl.ds(t0, tq), hs] = (acc_ref[i, h] * _silu(g)).astype(o_ref.dtype)
        return 0

    lax.fori_loop(0, nq, finish, 0)


def _sb_attn(proj, *, batch, seq, n_heads, col_q, col_k, col_v, col_g, nh, tq, tk):
    wblk = nh * HEAD_DIM
    nq = seq // tq
    tbl = _sb_sets(seq, tq)
    assert n_heads % nh == 0 and seq % tq == 0 and tq == 2 * tk and tk == V7X_LANES
    assert nq % 2 == 0 and tbl.shape[1] % 2 == 0

    def spec(col):
        base = col // wblk
        return pl.BlockSpec((seq, wblk), lambda b, hg, tbl_ref: (b, base + hg))

    blk = 5 * seq * wblk * 2 + 2 * nq * nh * tq * HEAD_DIM * 4 + nh * tq * tq * (2 * 4 + 2 * 2)
    kern = functools.partial(_sb_attn_kernel, nh=nh, tq=tq, tk=tk)
    return pl.pallas_call(
        kern,
        grid_spec=pltpu.PrefetchScalarGridSpec(
            num_scalar_prefetch=1,
            grid=(batch, n_heads // nh),
            in_specs=[spec(col_q), spec(col_k), spec(col_v), spec(col_g)],
            out_specs=pl.BlockSpec((seq, wblk), lambda b, hg, tbl_ref: (b, hg)),
            scratch_shapes=[pltpu.VMEM((nq + 1, nh, tq, HEAD_DIM), F32), pltpu.VMEM((nq, nh, tq, HEAD_DIM), F32),
                            pltpu.VMEM((2, nh, tq, tq), F32), pltpu.VMEM((2, nh, tq, tq), BF16),
                            pltpu.VMEM((tk, tk), F32)]),
        out_shape=jax.ShapeDtypeStruct((batch * seq, n_heads * HEAD_DIM), BF16),
        compiler_params=pltpu.CompilerParams(
            dimension_semantics=("parallel", "parallel"), vmem_limit_bytes=_vmem_limit(blk)),
        name="sb_attn",
    )(jnp.asarray(tbl), proj, proj, proj, proj)


_CH_TQ = 2 * CHUNK
_CH_WIN = (LEFT_CHUNKS + 2) * CHUNK
_CH_PAD = LEFT_CHUNKS * CHUNK
_CH_BASE = _CH_WIN + _CH_TQ


def _rel_base_index():
    u = np.arange(_CH_BASE)
    u = np.where(u >= _CH_WIN, u - _CH_BASE, u)
    return (np.clip(_CH_PAD - u, -(CHUNK - 1), REL_CLIP) + (CHUNK - 1)).astype(np.int32)


def _ch_attn_kernel(q_ref, k_ref, v_ref, g_ref, qg_ref, kg_ref, base_ref, o_ref,
                    kn_ref, bias_ref, *, scale):
    seq = q_ref.shape[0]
    tq, win = _CH_TQ, _CH_WIN
    nclip = _CH_PAD // tq
    nblk = seq // tq

    def rms(x, gain):
        ms = jnp.mean(x * x, axis=-1, keepdims=True)
        return x * lax.rsqrt(ms + NORM_EPS) * gain

    kn_ref[...] = rms(k_ref[...].astype(F32), kg_ref[...]).astype(BF16)
    q_gain = qg_ref[...] * (scale * LOG2_E)

    @pl.when(pl.program_id(1) == 0)
    def _():
        base = jnp.broadcast_to(base_ref[...] * LOG2_E, (tq, _CH_BASE))
        rolled = pltpu.roll(base, 0, 1, stride=1, stride_axis=0)[:, :win]
        qc = lax.broadcasted_iota(jnp.int32, (tq, win), 0) // CHUNK
        kc = lax.broadcasted_iota(jnp.int32, (tq, win), 1) // CHUNK
        bias_ref[...] = jnp.where((kc >= qc) & (kc <= qc + LEFT_CHUNKS), rolled, NEG_BIG)

    def blocks(t0s, k0s, widths, offs):
        def scores(n):
            qn = rms(q_ref[pl.ds(t0s[n], tq), :].astype(F32), q_gain).astype(BF16)
            kw = kn_ref[pl.ds(k0s[n], widths[n]), :]
            s = lax.dot_general(qn, kw, (((1,), (1,)), ((), ())), preferred_element_type=F32)
            return s + bias_ref[:, offs[n]:offs[n] + widths[n]]

        def softmax(s):
            p = jnp.exp2(s - jnp.max(s, axis=-1, keepdims=True))
            return p.astype(BF16), 1.0 / jnp.sum(p, axis=-1, keepdims=True)

        def output(n, p, inv_l):
            y = jnp.dot(p, v_ref[pl.ds(k0s[n], widths[n]), :], preferred_element_type=F32) * inv_l
            g = g_ref[pl.ds(t0s[n], tq), :].astype(F32)
            o_ref[pl.ds(t0s[n], tq), :] = (y * _silu(g)).astype(o_ref.dtype)

        ss = [scores(n) for n in range(len(t0s))]
        ps = [softmax(s) for s in ss]
        for n, (p, inv_l) in enumerate(ps):
            output(n, p, inv_l)

    for idx in (range(nclip), range(nclip, nblk)):
        blocks([i * tq for i in idx],
               [max(i - nclip, 0) * tq for i in idx],
               [min(i + 1, nclip + 1) * tq for i in idx],
               [max(nclip - i, 0) * tq for i in idx])


def _ch_attn(proj, q_gain, k_gain, rel_base, layer, *, batch, seq, n_heads, col_q, col_k, col_v, col_g):
    d = HEAD_DIM

    def spec(col):
        base = col // d
        return pl.BlockSpec((seq, d), lambda h, b: (b, base + h))

    assert _CH_PAD % _CH_TQ == 0 and seq % _CH_TQ == 0
    blk = 5 * seq * d * 2 + seq * d * 2 + _CH_TQ * _CH_WIN * 4
    kern = functools.partial(_ch_attn_kernel, scale=HEAD_DIM ** -0.5)
    return pl.pallas_call(
        kern,
        grid=(n_heads, batch),
        in_specs=[spec(col_q), spec(col_k), spec(col_v), spec(col_g),
                  pl.BlockSpec((None, 1, d), lambda h, b: (layer, 0, 0)),
                  pl.BlockSpec((None, 1, d), lambda h, b: (layer, 0, 0)),
                  pl.BlockSpec((None, None, 1, _CH_BASE), lambda h, b: (layer, h, 0, 0))],
        out_specs=pl.BlockSpec((seq, d), lambda h, b: (b, h)),
        out_shape=jax.ShapeDtypeStruct((batch * seq, n_heads * d), BF16),
        scratch_shapes=[pltpu.VMEM((seq, d), BF16), pltpu.VMEM((_CH_TQ, _CH_WIN), F32)],
        compiler_params=pltpu.CompilerParams(
            dimension_semantics=("parallel", "arbitrary"), vmem_limit_bytes=_vmem_limit(blk)),
        name="ch_attn",
    )(proj, proj, proj, proj, q_gain, k_gain, rel_base)


def kernel(x, norm_g, w_in, q_norm_g, k_norm_g, rel_bias, w_out):
    batch, seq, d_model = x.shape
    depth = w_in.shape[0]
    d_mix = w_out.shape[1]
    d_sb = d_mix // 2
    d_ch = d_mix - d_sb
    assert w_in.shape[2] == 4 * d_sb + 4 * d_ch and rel_bias.shape[2] == REL_CLIP + CHUNK
    assert seq % _CH_TQ == 0

    rel_base = rel_bias[:, :, _rel_base_index()][:, :, None, :]
    norm_g3 = norm_g[:, None, :]
    col_scale = np.ones((1, w_in.shape[2]), np.float32)
    col_scale[:, :d_sb] = HEAD_DIM ** -0.5 * LOG2_E
    q_gain3 = q_norm_g[:, None, :]
    k_gain3 = k_norm_g[:, None, :]

    x2 = x.reshape(batch * seq, d_model)
    for layer in range(depth):
        proj = _in_proj(x2, norm_g3, w_in, jnp.asarray(col_scale), layer, tm=1024, tn=1024)
        ma = _sb_attn(proj, batch=batch, seq=seq, n_heads=d_sb // HEAD_DIM,
                      col_q=0, col_k=d_sb, col_v=2 * d_sb, col_g=3 * d_sb, nh=4, tq=256, tk=128)
        c0 = 4 * d_sb
        mb = _ch_attn(proj, q_gain3, k_gain3, rel_base, layer,
                      batch=batch, seq=seq, n_heads=d_ch // HEAD_DIM,
                      col_q=c0, col_k=c0 + d_ch, col_v=c0 + 2 * d_ch, col_g=c0 + 3 * d_ch)
        x2 = _out_proj(x2, ma, mb, w_out, layer, tm=512, tn=d_model)
    return x2.reshape(batch, seq, d_model)
```

```python
import functools

import jax
import jax.numpy as jnp
import numpy as np
from jax import lax
from jax.experimental import pallas as pl
from jax.experimental.pallas import tpu as pltpu

HEAD_DIM = 128
CHUNK = 64
LEFT_CHUNKS = 8
REL_CLIP = 256
NORM_EPS = 1e-6
NEG_BIG = -1e30
LOG2_E = 1.4426950408889634

V7X_LANES = 128
V7X_VMEM_BYTES = 64 * 1024 * 1024

F32 = jnp.float32
BF16 = jnp.bfloat16


_VMEM_CAP = V7X_VMEM_BYTES - 8 * 1024 * 1024
_VMEM_TEMPS = 24 * 1024 * 1024


def _vmem_limit(block_bytes):
    return int(min(2 * block_bytes + _VMEM_TEMPS, _VMEM_CAP))


_NORM_SLICES = 8

def _in_proj_kernel(x_ref, g_ref, w_ref, cs_ref, o_ref, h_ref):
    rows = x_ref.shape[0] // _NORM_SLICES

    @pl.when(pl.program_id(1) == 0)
    def _():
        w = w_ref[...].astype(BF16)
        for r in range(_NORM_SLICES):
            sl = pl.ds(r * rows, rows)
            x = x_ref[sl, :]
            ms = jnp.mean(x * x, axis=-1, keepdims=True)
            h = (x * lax.rsqrt(ms + NORM_EPS) * g_ref[...]).astype(BF16)
            h_ref[sl, :] = h
            o_ref[sl, :] = (jnp.dot(h, w, preferred_element_type=F32) * cs_ref[...]).astype(o_ref.dtype)

    @pl.when(pl.program_id(1) != 0)
    def _():
        w = w_ref[...].astype(BF16)
        o_ref[...] = (jnp.dot(h_ref[...], w, preferred_element_type=F32) * cs_ref[...]).astype(o_ref.dtype)


def _in_proj(x2, g, w_in, col_scale, layer, *, tm, tn):
    m, d = x2.shape
    n = w_in.shape[2]
    blk = tm * d * 4 + d * tn * 4 + tm * tn * 2 + tm * d * 2
    return pl.pallas_call(
        _in_proj_kernel,
        grid=(m // tm, n // tn),
        in_specs=[
            pl.BlockSpec((tm, d), lambda i, j: (i, 0)),
            pl.BlockSpec((None, 1, d), lambda i, j: (layer, 0, 0)),
            pl.BlockSpec((None, d, tn), lambda i, j: (layer, 0, j)),
            pl.BlockSpec((1, tn), lambda i, j: (0, j)),
        ],
        out_specs=pl.BlockSpec((tm, tn), lambda i, j: (i, j)),
        out_shape=jax.ShapeDtypeStruct((m, n), BF16),
        scratch_shapes=[pltpu.VMEM((tm, d), BF16)],
        compiler_params=pltpu.CompilerParams(
            dimension_semantics=("parallel", "arbitrary"), vmem_limit_bytes=_vmem_limit(blk)),
        name="in_proj",
    )(x2, g, w_in, col_scale)


def _out_proj_kernel(x_ref, ma_ref, mb_ref, w_ref, o_ref, wb_ref):
    @pl.when(pl.program_id(1) == 0)
    def _():
        wb_ref[...] = w_ref[...].astype(BF16)

    da = ma_ref.shape[1]
    acc = jnp.dot(ma_ref[...], wb_ref[:da, :], preferred_element_type=F32)
    acc = acc + jnp.dot(mb_ref[...], wb_ref[da:, :], preferred_element_type=F32)
    o_ref[...] = x_ref[...] + acc


def _out_proj(x2, ma, mb, w_out, layer, *, tm, tn):
    m, d = x2.shape
    da, db = ma.shape[1], mb.shape[1]
    blk = tm * tn * 4 * 2 + tm * (da + db) * 2 + (da + db) * tn * 4 + (da + db) * tn * 2
    return pl.pallas_call(
        _out_proj_kernel,
        grid=(d // tn, m // tm),
        in_specs=[
            pl.BlockSpec((tm, tn), lambda j, i: (i, j)),
            pl.BlockSpec((tm, da), lambda j, i: (i, 0)),
            pl.BlockSpec((tm, db), lambda j, i: (i, 0)),
            pl.BlockSpec((None, da + db, tn), lambda j, i: (layer, 0, j), pipeline_mode=pl.Buffered(1)),
        ],
        out_specs=pl.BlockSpec((tm, tn), lambda j, i: (i, j)),
        out_shape=jax.ShapeDtypeStruct((m, d), F32),
        scratch_shapes=[pltpu.VMEM((da + db, tn), BF16)],
        compiler_params=pltpu.CompilerParams(
            dimension_semantics=("parallel", "arbitrary"), vmem_limit_bytes=_vmem_limit(blk)),
        name="out_proj",
    )(x2, ma, mb, w_out)


def _silu(g):
    return g * (1.0 / (1.0 + jnp.exp(-g)))


def _sb_sets(seq, tq):
    rows = [(i * tq, (i - jj) * tq, i) for i in range(seq // tq) for jj in range(1, i + 1)]
    return np.asarray(rows, np.int32).T.copy()


def _sb_attn_kernel(tbl_ref, q_ref, k_ref, v_ref, g_ref, o_ref,
                    acc_ref, car_ref, z_ref, w_ref, bm_ref, *, nh, tq, tk):
    seq = q_ref.shape[0]
    d = HEAD_DIM
    nq = seq // tq
    nsets = tbl_ref.shape[1]

    row = lax.broadcasted_iota(jnp.int32, (2 * tk, 2 * tk), 0) % tk
    col = lax.broadcasted_iota(jnp.int32, (2 * tk, 2 * tk), 1)
    uj = jnp.where((col >= tk) | (row > col), 1.0, 0.0).astype(BF16)
    t_idx = lax.broadcasted_iota(jnp.int32, (tk, tk), 0)
    s_idx = lax.broadcasted_iota(jnp.int32, (tk, tk), 1)
    bm_ref[...] = jnp.where(s_idx < t_idx, 0.0, NEG_BIG)
    w_ref[...] = jnp.zeros_like(w_ref)

    heads = [pl.ds(h * d, d) for h in range(nh)]

    def split(z):
        sp = jnp.maximum(z, 0.0) + jnp.log2(1.0 + jnp.exp2(-jnp.abs(z)))
        hi = sp.astype(BF16)
        lo = (sp - hi.astype(F32)).astype(BF16)
        return z - sp, jnp.concatenate([hi, lo], axis=1)

    def qk_stage(t0, k0, slot):
        for h, hs in enumerate(heads):
            z_ref[slot, h] = lax.dot_general(q_ref[pl.ds(t0, tq), hs], k_ref[pl.ds(k0, tq), hs],
                                             (((1,), (1,)), ((), ())), preferred_element_type=F32)

    def pv_stage(k0, qi, slot, init):
        for h, hs in enumerate(heads):
            y = jnp.dot(w_ref[slot, h], v_ref[pl.ds(k0, tq), hs], preferred_element_type=F32)
            if init:
                acc_ref[qi, h] = y
            else:
                acc_ref[qi, h] += y

    def mid_diag(qi, zslot, wslot):
        parts = []
        for h in range(nh):
            zl = z_ref[zslot, h, :, :tk]
            zl = jnp.concatenate([zl[:tk] + bm_ref[...], zl[tk:]], axis=0)
            zr = z_ref[zslot, h, tk:, tk:] + bm_ref[...]
            parts.append(split(zl) + split(zr))
        for h in range(nh):
            zs_l, cat_l, zs_r, cat_r = parts[h]
            cs_r = jnp.dot(cat_r, uj, preferred_element_type=F32)
            cs_l = jnp.dot(cat_l, uj, preferred_element_type=F32)
            tot_r = cs_r[:, tk:]
            w_ref[wslot, h, tk:, tk:] = jnp.exp2(zs_r - cs_r[:, :tk]).astype(BF16)
            w_ref[wslot, h, :tk, :tk] = jnp.exp2(zs_l[:tk] - cs_l[:tk, :tk]).astype(BF16)
            w_ref[wslot, h, tk:, :tk] = jnp.exp2(zs_l[tk:] - cs_l[tk:, :tk] - tot_r).astype(BF16)
            car_ref[qi, h, :tk] = cs_l[:tk, tk:]
            car_ref[qi, h, tk:] = tot_r + cs_l[tk:, tk:]

    def mid_full(qi, zslot, wslot):
        parts = []
        for h in range(nh):
            for b in range(2):
                parts.append(split(z_ref[zslot, h, :, pl.ds((1 - b) * tk, tk)]))
        for h in range(nh):
            car = car_ref[qi, h]
            for b in range(2):
                zs, cat = parts[2 * h + b]
                cs = jnp.dot(cat, uj, preferred_element_type=F32)
                w_ref[wslot, h, :, pl.ds((1 - b) * tk, tk)] = jnp.exp2(zs - cs[:, :tk] - car).astype(BF16)
                car = car + cs[:, tk:]
            car_ref[qi, h] = car

    def pipeline(count, qk, pv, mid):
        qk(0, 0)

        def two_sets(m, _):
            n = 2 * m
            qk(n + 1, 1)
            pv(n - 1, 1)
            mid(n, 0, 0)
            qk(jnp.minimum(n + 2, count - 1), 0)
            pv(n, 0)
            mid(n + 1, 1, 1)
            return 0

        lax.fori_loop(0, count // 2, two_sets, 0)
        pv(count - 1, 1)

    def row0(i):
        return pl.multiple_of(i * tq, tq)

    def spare(n, qi):
        return jnp.where(n < 0, nq, qi)

    pipeline(nq,
             lambda i, slot: qk_stage(row0(i), row0(i), slot),
             lambda i, slot: pv_stage(row0(jnp.maximum(i, 0)), spare(i, i), slot, True),
             mid_diag)

    def key0(n):
        return pl.multiple_of(tbl_ref[1, jnp.maximum(n, 0)], tq)

    pipeline(nsets,
             lambda n, slot: qk_stage(pl.multiple_of(tbl_ref[0, n], tq), key0(n), slot),
             lambda n, slot: pv_stage(key0(n), spare(n, tbl_ref[2, jnp.maximum(n, 0)]), slot, False),
             lambda n, zslot, wslot: mid_full(tbl_ref[2, n], zslot, wslot))

    def finish(i, _):
        t0 = row0(i)
        for h, hs in enumerate(heads):
            g = g_ref[pl.ds(t0, tq), hs].astype(F32)
            o_ref[pl.ds(t0, tq), hs] = (acc_ref[i, h] * _silu(g)).astype(o_ref.dtype)
        return 0

    lax.fori_loop(0, nq, finish, 0)


def _sb_attn(proj, *, batch, seq, n_heads, col_q, col_k, col_v, col_g, nh, tq, tk):
    wblk = nh * HEAD_DIM
    nq = seq // tq
    tbl = _sb_sets(seq, tq)
    assert n_heads % nh == 0 and seq % tq == 0 and tq == 2 * tk and tk == V7X_LANES
    assert nq % 2 == 0 and tbl.shape[1] % 2 == 0

    def spec(col):
        base = col // wblk
        return pl.BlockSpec((seq, wblk), lambda b, hg, tbl_ref: (b, base + hg))

    blk = 5 * seq * wblk * 2 + 2 * nq * nh * tq * HEAD_DIM * 4 + nh * tq * tq * (2 * 4 + 2 * 2)
    kern = functools.partial(_sb_attn_kernel, nh=nh, tq=tq, tk=tk)
    return pl.pallas_call(
        kern,
        grid_spec=pltpu.PrefetchScalarGridSpec(
            num_scalar_prefetch=1,
            grid=(batch, n_heads // nh),
            in_specs=[spec(col_q), spec(col_k), spec(col_v), spec(col_g)],
            out_specs=pl.BlockSpec((seq, wblk), lambda b, hg, tbl_ref: (b, hg)),
            scratch_shapes=[pltpu.VMEM((nq + 1, nh, tq, HEAD_DIM), F32), pltpu.VMEM((nq, nh, tq, HEAD_DIM), F32),
                            pltpu.VMEM((2, nh, tq, tq), F32), pltpu.VMEM((2, nh, tq, tq), BF16),
                            pltpu.VMEM((tk, tk), F32)]),
        out_shape=jax.ShapeDtypeStruct((batch * seq, n_heads * HEAD_DIM), BF16),
        compiler_params=pltpu.CompilerParams(
            dimension_semantics=("parallel", "parallel"), vmem_limit_bytes=_vmem_limit(blk)),
        name="sb_attn",
    )(jnp.asarray(tbl), proj, proj, proj, proj)


_CH_TQ = 2 * CHUNK
_CH_WIN = (LEFT_CHUNKS + 2) * CHUNK
_CH_PAD = LEFT_CHUNKS * CHUNK
_CH_BASE = _CH_WIN + _CH_TQ


def _rel_base_index():
    u = np.arange(_CH_BASE)
    u = np.where(u >= _CH_WIN, u - _CH_BASE, u)
    return (np.clip(_CH_PAD - u, -(CHUNK - 1), REL_CLIP) + (CHUNK - 1)).astype(np.int32)


def _ch_attn_kernel(q_ref, k_ref, v_ref, g_ref, qg_ref, kg_ref, base_ref, o_ref,
                    kn_ref, bias_ref, *, scale):
    seq = q_ref.shape[0]
    tq, win = _CH_TQ, _CH_WIN
    nclip = _CH_PAD // tq
    nblk = seq // tq

    def rms(x, gain):
        ms = jnp.mean(x * x, axis=-1, keepdims=True)
        return x * lax.rsqrt(ms + NORM_EPS) * gain

    kn_ref[...] = rms(k_ref[...].astype(F32), kg_ref[...]).astype(BF16)
    q_gain = qg_ref[...] * (scale * LOG2_E)

    @pl.when(pl.program_id(1) == 0)
    def _():
        base = jnp.broadcast_to(base_ref[...] * LOG2_E, (tq, _CH_BASE))
        rolled = pltpu.roll(base, 0, 1, stride=1, stride_axis=0)[:, :win]
        qc = lax.broadcasted_iota(jnp.int32, (tq, win), 0) // CHUNK
        kc = lax.broadcasted_iota(jnp.int32, (tq, win), 1) // CHUNK
        bias_ref[...] = jnp.where((kc >= qc) & (kc <= qc + LEFT_CHUNKS), rolled, NEG_BIG)

    def blocks(t0s, k0s, widths, offs):
        def scores(n):
            qn = rms(q_ref[pl.ds(t0s[n], tq), :].astype(F32), q_gain).astype(BF16)
            kw = kn_ref[pl.ds(k0s[n], widths[n]), :]
            s = lax.dot_general(qn, kw, (((1,), (1,)), ((), ())), preferred_element_type=F32)
            return s + bias_ref[:, offs[n]:offs[n] + widths[n]]

        def softmax(s):
            p = jnp.exp2(s - jnp.max(s, axis=-1, keepdims=True))
            return p.astype(BF16), 1.0 / jnp.sum(p, axis=-1, keepdims=True)

        def output(n, p, inv_l):
            y = jnp.dot(p, v_ref[pl.ds(k0s[n], widths[n]), :], preferred_element_type=F32) * inv_l
            g = g_ref[pl.ds(t0s[n], tq), :].astype(F32)
            o_ref[pl.ds(t0s[n], tq), :] = (y * _silu(g)).astype(o_ref.dtype)

        ss = [scores(n) for n in range(len(t0s))]
        ps = [softmax(s) for s in ss]
        for n, (p, inv_l) in enumerate(ps):
            output(n, p, inv_l)

    for idx in (range(nclip), range(nclip, nblk)):
        blocks([i * tq for i in idx],
               [max(i - nclip, 0) * tq for i in idx],
               [min(i + 1, nclip + 1) * tq for i in idx],
               [max(nclip - i, 0) * tq for i in idx])


def _ch_attn(proj, q_gain, k_gain, rel_base, layer, *, batch, seq, n_heads, col_q, col_k, col_v, col_g):
    d = HEAD_DIM

    def spec(col):
        base = col // d
        return pl.BlockSpec((seq, d), lambda h, b: (b, base + h))

    assert _CH_PAD % _CH_TQ == 0 and seq % _CH_TQ == 0
    kern = functools.partial(_ch_attn_kernel, scale=HEAD_DIM ** -0.5)
    return pl.pallas_call(
        kern,
        grid=(n_heads, batch),
        in_specs=[spec(col_q), spec(col_k), spec(col_v), spec(col_g),
                  pl.BlockSpec((None, 1, d), lambda h, b: (layer, 0, 0)),
                  pl.BlockSpec((None, 1, d), lambda h, b: (layer, 0, 0)),
                  pl.BlockSpec((None, None, 1, _CH_BASE), lambda h, b: (layer, h, 0, 0))],
        out_specs=pl.BlockSpec((seq, d), lambda h, b: (b, h)),
        out_shape=jax.ShapeDtypeStruct((batch * seq, n_heads * d), BF16),
        scratch_shapes=[pltpu.VMEM((seq, d), BF16), pltpu.VMEM((_CH_TQ, _CH_WIN), F32)],
        compiler_params=pltpu.CompilerParams(
            dimension_semantics=("parallel", "arbitrary"), vmem_limit_bytes=_VMEM_CAP),
        name="ch_attn",
    )(proj, proj, proj, proj, q_gain, k_gain, rel_base)


def kernel(x, norm_g, w_in, q_norm_g, k_norm_g, rel_bias, w_out):
    batch, seq, d_model = x.shape
    depth = w_in.shape[0]
    d_mix = w_out.shape[1]
    d_sb = d_mix // 2
    d_ch = d_mix - d_sb
    assert w_in.shape[2] == 4 * d_sb + 4 * d_ch and rel_bias.shape[2] == REL_CLIP + CHUNK
    assert seq % _CH_TQ == 0

    rel_base = rel_bias[:, :, _rel_base_index()][:, :, None, :]
    norm_g3 = norm_g[:, None, :]
    col_scale = np.ones((1, w_in.shape[2]), np.float32)
    col_scale[:, :d_sb] = HEAD_DIM ** -0.5 * LOG2_E
    q_gain3 = q_norm_g[:, None, :]
    k_gain3 = k_norm_g[:, None, :]

    x2 = x.reshape(batch * seq, d_model)
    for layer in range(depth):
        proj = _in_proj(x2, norm_g3, w_in, jnp.asarray(col_scale), layer, tm=1024, tn=1024)
        ma = _sb_attn(proj, batch=batch, seq=seq, n_heads=d_sb // HEAD_DIM,
                      col_q=0, col_k=d_sb, col_v=2 * d_sb, col_g=3 * d_sb, nh=4, tq=256, tk=128)
        c0 = 4 * d_sb
        mb = _ch_attn(proj, q_gain3, k_gain3, rel_base, layer,
                      batch=batch, seq=seq, n_heads=d_ch // HEAD_DIM,
                      col_q=c0, col_k=c0 + d_ch, col_v=c0 + 2 * d_ch, col_g=c0 + 3 * d_ch)
        x2 = _out_proj(x2, ma, mb, w_out, layer, tm=512, tn=d_model)
    return x2.reshape(batch, seq, d_model)
```

```python
import functools

import jax
import jax.numpy as jnp
import numpy as np
from jax import lax
from jax.experimental import pallas as pl
from jax.experimental.pallas import tpu as pltpu

HEAD_DIM = 128
CHUNK = 64
LEFT_CHUNKS = 8
REL_CLIP = 256
NORM_EPS = 1e-6
NEG_BIG = -1e30
LOG2_E = 1.4426950408889634

V7X_LANES = 128
V7X_VMEM_BYTES = 64 * 1024 * 1024

F32 = jnp.float32
BF16 = jnp.bfloat16


_VMEM_CAP = V7X_VMEM_BYTES - 8 * 1024 * 1024
_VMEM_TEMPS = 24 * 1024 * 1024


def _vmem_limit(block_bytes):
    return int(min(2 * block_bytes + _VMEM_TEMPS, _VMEM_CAP))


_NORM_SLICES = 8

def _in_proj_kernel(x_ref, g_ref, w_ref, cs_ref, o_ref, h_ref):
    rows = x_ref.shape[0] // _NORM_SLICES

    @pl.when(pl.program_id(1) == 0)
    def _():
        w = w_ref[...].astype(BF16)
        for r in range(_NORM_SLICES):
            sl = pl.ds(r * rows, rows)
            x = x_ref[sl, :]
            ms = jnp.mean(x * x, axis=-1, keepdims=True)
            h = (x * lax.rsqrt(ms + NORM_EPS) * g_ref[...]).astype(BF16)
            h_ref[sl, :] = h
            o_ref[sl, :] = (jnp.dot(h, w, preferred_element_type=F32) * cs_ref[...]).astype(o_ref.dtype)

    @pl.when(pl.program_id(1) != 0)
    def _():
        w = w_ref[...].astype(BF16)
        o_ref[...] = (jnp.dot(h_ref[...], w, preferred_element_type=F32) * cs_ref[...]).astype(o_ref.dtype)


def _in_proj(x2, g, w_in, col_scale, layer, *, tm, tn):
    m, d = x2.shape
    n = w_in.shape[2]
    blk = tm * d * 4 + d * tn * 4 + tm * tn * 2 + tm * d * 2
    return pl.pallas_call(
        _in_proj_kernel,
        grid=(m // tm, n // tn),
        in_specs=[
            pl.BlockSpec((tm, d), lambda i, j: (i, 0)),
            pl.BlockSpec((None, 1, d), lambda i, j: (layer, 0, 0)),
            pl.BlockSpec((None, d, tn), lambda i, j: (layer, 0, j)),
            pl.BlockSpec((1, tn), lambda i, j: (0, j)),
        ],
        out_specs=pl.BlockSpec((tm, tn), lambda i, j: (i, j)),
        out_shape=jax.ShapeDtypeStruct((m, n), BF16),
        scratch_shapes=[pltpu.VMEM((tm, d), BF16)],
        compiler_params=pltpu.CompilerParams(
            dimension_semantics=("parallel", "arbitrary"), vmem_limit_bytes=_vmem_limit(blk)),
        name="in_proj",
    )(x2, g, w_in, col_scale)


_GATE_SLICES = 4


def _silu(g):
    return g * (1.0 / (1.0 + jnp.exp(-g)))


def _out_proj_kernel(x_ref, ya_ref, yb_ref, ga_ref, gb_ref, w_ref, o_ref, wb_ref):
    @pl.when(pl.program_id(1) == 0)
    def _():
        wb_ref[...] = w_ref[...].astype(BF16)

    da = ya_ref.shape[1]
    rows = x_ref.shape[0] // _GATE_SLICES
    for r in range(_GATE_SLICES):
        sl = pl.ds(r * rows, rows)
        ma = (ya_ref[sl, :].astype(F32) * _silu(ga_ref[sl, :].astype(F32))).astype(BF16)
        mb = (yb_ref[sl, :].astype(F32) * _silu(gb_ref[sl, :].astype(F32))).astype(BF16)
        acc = jnp.dot(ma, wb_ref[:da, :], preferred_element_type=F32)
        acc = acc + jnp.dot(mb, wb_ref[da:, :], preferred_element_type=F32)
        o_ref[sl, :] = x_ref[sl, :] + acc


def _out_proj(x2, ya, yb, proj, col_ga, col_gb, w_out, layer, *, tm, tn):
    m, d = x2.shape
    da, db = ya.shape[1], yb.shape[1]
    assert col_ga % da == 0 and col_gb % db == 0
    blk = tm * tn * 4 * 2 + 2 * tm * (da + db) * 2 + (da + db) * tn * 4 + (da + db) * tn * 2
    return pl.pallas_call(
        _out_proj_kernel,
        grid=(d // tn, m // tm),
        in_specs=[
            pl.BlockSpec((tm, tn), lambda j, i: (i, j)),
            pl.BlockSpec((tm, da), lambda j, i: (i, 0)),
            pl.BlockSpec((tm, db), lambda j, i: (i, 0)),
            pl.BlockSpec((tm, da), lambda j, i: (i, col_ga // da)),
            pl.BlockSpec((tm, db), lambda j, i: (i, col_gb // db)),
            pl.BlockSpec((None, da + db, tn), lambda j, i: (layer, 0, j), pipeline_mode=pl.Buffered(1)),
        ],
        out_specs=pl.BlockSpec((tm, tn), lambda j, i: (i, j)),
        out_shape=jax.ShapeDtypeStruct((m, d), F32),
        scratch_shapes=[pltpu.VMEM((da + db, tn), BF16)],
        compiler_params=pltpu.CompilerParams(
            dimension_semantics=("parallel", "arbitrary"), vmem_limit_bytes=_vmem_limit(blk)),
        name="out_proj",
    )(x2, ya, yb, proj, proj, w_out)


def _sb_sets(seq, tq):
    rows = [(i * tq, (i - jj) * tq, i) for i in range(seq // tq) for jj in range(1, i + 1)]
    return np.asarray(rows, np.int32).T.copy()


def _sb_attn_kernel(tbl_ref, q_ref, k_ref, v_ref, o_ref,
                    acc_ref, car_ref, z_ref, w_ref, bm_ref, *, nh, tq, tk):
    seq = q_ref.shape[0]
    d = HEAD_DIM
    nq = seq // tq
    nsets = tbl_ref.shape[1]

    row = lax.broadcasted_iota(jnp.int32, (2 * tk, 2 * tk), 0) % tk
    col = lax.broadcasted_iota(jnp.int32, (2 * tk, 2 * tk), 1)
    uj = jnp.where((col >= tk) | (row > col), 1.0, 0.0).astype(BF16)
    t_idx = lax.broadcasted_iota(jnp.int32, (tk, tk), 0)
    s_idx = lax.broadcasted_iota(jnp.int32, (tk, tk), 1)
    bm_ref[...] = jnp.where(s_idx < t_idx, 0.0, NEG_BIG)
    w_ref[...] = jnp.zeros_like(w_ref)

    heads = [pl.ds(h * d, d) for h in range(nh)]

    def split(z):
        sp = jnp.maximum(z, 0.0) + jnp.log2(1.0 + jnp.exp2(-jnp.abs(z)))
        hi = sp.astype(BF16)
        lo = (sp - hi.astype(F32)).astype(BF16)
        return z - sp, jnp.concatenate([hi, lo], axis=1)

    def qk_stage(t0, k0, slot):
        for h, hs in enumerate(heads):
            z_ref[slot, h] = lax.dot_general(q_ref[pl.ds(t0, tq), hs], k_ref[pl.ds(k0, tq), hs],
                                             (((1,), (1,)), ((), ())), preferred_element_type=F32)

    def pv_stage(k0, qi, slot, init):
        for h, hs in enumerate(heads):
            y = jnp.dot(w_ref[slot, h], v_ref[pl.ds(k0, tq), hs], preferred_element_type=F32)
            if init:
                acc_ref[qi, h] = y
            else:
                acc_ref[qi, h] += y

    def mid_diag(qi, zslot, wslot):
        parts = []
        for h in range(nh):
            zl = z_ref[zslot, h, :, :tk]
            zl = jnp.concatenate([zl[:tk] + bm_ref[...], zl[tk:]], axis=0)
            zr = z_ref[zslot, h, tk:, tk:] + bm_ref[...]
            parts.append(split(zl) + split(zr))
        for h in range(nh):
            zs_l, cat_l, zs_r, cat_r = parts[h]
            cs_r = jnp.dot(cat_r, uj, preferred_element_type=F32)
            cs_l = jnp.dot(cat_l, uj, preferred_element_type=F32)
            tot_r = cs_r[:, tk:]
            w_ref[wslot, h, tk:, tk:] = jnp.exp2(zs_r - cs_r[:, :tk]).astype(BF16)
            w_ref[wslot, h, :tk, :tk] = jnp.exp2(zs_l[:tk] - cs_l[:tk, :tk]).astype(BF16)
            w_ref[wslot, h, tk:, :tk] = jnp.exp2(zs_l[tk:] - cs_l[tk:, :tk] - tot_r).astype(BF16)
            car_ref[qi, h, :tk] = cs_l[:tk, tk:]
            car_ref[qi, h, tk:] = tot_r + cs_l[tk:, tk:]

    def mid_full(qi, zslot, wslot):
        parts = []
        for h in range(nh):
            for b in range(2):
                parts.append(split(z_ref[zslot, h, :, pl.ds((1 - b) * tk, tk)]))
        for h in range(nh):
            car = car_ref[qi, h]
            for b in range(2):
                zs, cat = parts[2 * h + b]
                cs = jnp.dot(cat, uj, preferred_element_type=F32)
                w_ref[wslot, h, :, pl.ds((1 - b) * tk, tk)] = jnp.exp2(zs - cs[:, :tk] - car).astype(BF16)
                car = car + cs[:, tk:]
            car_ref[qi, h] = car

    def pipeline(count, qk, pv, mid):
        qk(0, 0)

        def two_sets(m, _):
            n = 2 * m
            qk(n + 1, 1)
            pv(n - 1, 1)
            mid(n, 0, 0)
            qk(jnp.minimum(n + 2, count - 1), 0)
            pv(n, 0)
            mid(n + 1, 1, 1)
            return 0

        lax.fori_loop(0, count // 2, two_sets, 0)
        pv(count - 1, 1)

    def row0(i):
        return pl.multiple_of(i * tq, tq)

    def spare(n, qi):
        return jnp.where(n < 0, nq, qi)

    pipeline(nq,
             lambda i, slot: qk_stage(row0(i), row0(i), slot),
             lambda i, slot: pv_stage(row0(jnp.maximum(i, 0)), spare(i, i), slot, True),
             mid_diag)

    def key0(n):
        return pl.multiple_of(tbl_ref[1, jnp.maximum(n, 0)], tq)

    pipeline(nsets,
             lambda n, slot: qk_stage(pl.multiple_of(tbl_ref[0, n], tq), key0(n), slot),
             lambda n, slot: pv_stage(key0(n), spare(n, tbl_ref[2, jnp.maximum(n, 0)]), slot, False),
             lambda n, zslot, wslot: mid_full(tbl_ref[2, n], zslot, wslot))

    def finish(i, _):
        t0 = row0(i)
        for h, hs in enumerate(heads):
            o_ref[pl.ds(t0, tq), hs] = acc_ref[i, h].astype(o_ref.dtype)
        return 0

    lax.fori_loop(0, nq, finish, 0)


def _sb_attn(proj, *, batch, seq, n_heads, col_q, col_k, col_v, nh, tq, tk):
    wblk = nh * HEAD_DIM
    nq = seq // tq
    tbl = _sb_sets(seq, tq)
    assert n_heads % nh == 0 and seq % tq == 0 and tq == 2 * tk and tk == V7X_LANES
    assert nq % 2 == 0 and tbl.shape[1] % 2 == 0

    def spec(col):
        base = col // wblk
        return pl.BlockSpec((seq, wblk), lambda b, hg, tbl_ref: (b, base + hg))

    blk = 4 * seq * wblk * 2 + 2 * nq * nh * tq * HEAD_DIM * 4 + nh * tq * tq * (2 * 4 + 2 * 2)
    kern = functools.partial(_sb_attn_kernel, nh=nh, tq=tq, tk=tk)
    return pl.pallas_call(
        kern,
        grid_spec=pltpu.PrefetchScalarGridSpec(
            num_scalar_prefetch=1,
            grid=(batch, n_heads // nh),
            in_specs=[spec(col_q), spec(col_k), spec(col_v)],
            out_specs=pl.BlockSpec((seq, wblk), lambda b, hg, tbl_ref: (b, hg)),
            scratch_shapes=[pltpu.VMEM((nq + 1, nh, tq, HEAD_DIM), F32), pltpu.VMEM((nq, nh, tq, HEAD_DIM), F32),
                            pltpu.VMEM((2, nh, tq, tq), F32), pltpu.VMEM((2, nh, tq, tq), BF16),
                            pltpu.VMEM((tk, tk), F32)]),
        out_shape=jax.ShapeDtypeStruct((batch * seq, n_heads * HEAD_DIM), BF16),
        compiler_params=pltpu.CompilerParams(
            dimension_semantics=("parallel", "parallel"), vmem_limit_bytes=_vmem_limit(blk)),
        name="sb_attn",
    )(jnp.asarray(tbl), proj, proj, proj)


_CH_TQ = 2 * CHUNK
_CH_WIN = (LEFT_CHUNKS + 2) * CHUNK
_CH_PAD = LEFT_CHUNKS * CHUNK
_CH_BASE = _CH_WIN + _CH_TQ


def _rel_base_index():
    u = np.arange(_CH_BASE)
    u = np.where(u >= _CH_WIN, u - _CH_BASE, u)
    return (np.clip(_CH_PAD - u, -(CHUNK - 1), REL_CLIP) + (CHUNK - 1)).astype(np.int32)


def _ch_attn_kernel(q_ref, k_ref, v_ref, qg_ref, kg_ref, base_ref, o_ref,
                    kn_ref, bias_ref, *, scale):
    seq = q_ref.shape[0]
    tq, win = _CH_TQ, _CH_WIN
    nclip = _CH_PAD // tq
    nblk = seq // tq

    def rms(x, gain):
        ms = jnp.mean(x * x, axis=-1, keepdims=True)
        return x * lax.rsqrt(ms + NORM_EPS) * gain

    kn_ref[...] = rms(k_ref[...].astype(F32), kg_ref[...]).astype(BF16)
    q_gain = qg_ref[...] * (scale * LOG2_E)

    @pl.when(pl.program_id(1) == 0)
    def _():
        base = jnp.broadcast_to(base_ref[...] * LOG2_E, (tq, _CH_BASE))
        rolled = pltpu.roll(base, 0, 1, stride=1, stride_axis=0)[:, :win]
        qc = lax.broadcasted_iota(jnp.int32, (tq, win), 0) // CHUNK
        kc = lax.broadcasted_iota(jnp.int32, (tq, win), 1) // CHUNK
        bias_ref[...] = jnp.where((kc >= qc) & (kc <= qc + LEFT_CHUNKS), rolled, NEG_BIG)

    def blocks(t0s, k0s, widths, offs):
        def scores(n):
            qn = rms(q_ref[pl.ds(t0s[n], tq), :].astype(F32), q_gain).astype(BF16)
            kw = kn_ref[pl.ds(k0s[n], widths[n]), :]
            s = lax.dot_general(qn, kw, (((1,), (1,)), ((), ())), preferred_element_type=F32)
            return s + bias_ref[:, offs[n]:offs[n] + widths[n]]

        def softmax(s):
            p = jnp.exp2(s - jnp.max(s, axis=-1, keepdims=True))
            return p.astype(BF16), 1.0 / jnp.sum(p, axis=-1, keepdims=True)

        def output(n, p, inv_l):
            y = jnp.dot(p, v_ref[pl.ds(k0s[n], widths[n]), :], preferred_element_type=F32) * inv_l
            o_ref[pl.ds(t0s[n], tq), :] = y.astype(o_ref.dtype)

        ss = [scores(n) for n in range(len(t0s))]
        ps = [softmax(s) for s in ss]
        for n, (p, inv_l) in enumerate(ps):
            output(n, p, inv_l)

    for idx in (range(nclip), range(nclip, nblk)):
        blocks([i * tq for i in idx],
               [max(i - nclip, 0) * tq for i in idx],
               [min(i + 1, nclip + 1) * tq for i in idx],
               [max(nclip - i, 0) * tq for i in idx])


def _ch_attn(proj, q_gain, k_gain, rel_base, layer, *, batch, seq, n_heads, col_q, col_k, col_v):
    d = HEAD_DIM

    def spec(col):
        base = col // d
        return pl.BlockSpec((seq, d), lambda h, b: (b, base + h))

    assert _CH_PAD % _CH_TQ == 0 and seq % _CH_TQ == 0
    kern = functools.partial(_ch_attn_kernel, scale=HEAD_DIM ** -0.5)
    return pl.pallas_call(
        kern,
        grid=(n_heads, batch),
        in_specs=[spec(col_q), spec(col_k), spec(col_v),
                  pl.BlockSpec((None, 1, d), lambda h, b: (layer, 0, 0)),
                  pl.BlockSpec((None, 1, d), lambda h, b: (layer, 0, 0)),
                  pl.BlockSpec((None, None, 1, _CH_BASE), lambda h, b: (layer, h, 0, 0))],
        out_specs=pl.BlockSpec((seq, d), lambda h, b: (b, h)),
        out_shape=jax.ShapeDtypeStruct((batch * seq, n_heads * d), BF16),
        scratch_shapes=[pltpu.VMEM((seq, d), BF16), pltpu.VMEM((_CH_TQ, _CH_WIN), F32)],
        compiler_params=pltpu.CompilerParams(
            dimension_semantics=("parallel", "arbitrary"), vmem_limit_bytes=_VMEM_CAP),
        name="ch_attn",
    )(proj, proj, proj, q_gain, k_gain, rel_base)


def kernel(x, norm_g, w_in, q_norm_g, k_norm_g, rel_bias, w_out):
    batch, seq, d_model = x.shape
    depth = w_in.shape[0]
    d_mix = w_out.shape[1]
    d_sb = d_mix // 2
    d_ch = d_mix - d_sb
    assert w_in.shape[2] == 4 * d_sb + 4 * d_ch and rel_bias.shape[2] == REL_CLIP + CHUNK
    assert seq % _CH_TQ == 0

    rel_base = rel_bias[:, :, _rel_base_index()][:, :, None, :]
    norm_g3 = norm_g[:, None, :]
    col_scale = np.ones((1, w_in.shape[2]), np.float32)
    col_scale[:, :d_sb] = HEAD_DIM ** -0.5 * LOG2_E
    q_gain3 = q_norm_g[:, None, :]
    k_gain3 = k_norm_g[:, None, :]

    x2 = x.reshape(batch * seq, d_model)
    for layer in range(depth):
        proj = _in_proj(x2, norm_g3, w_in, jnp.asarray(col_scale), layer, tm=1024, tn=1024)
        ya = _sb_attn(proj, batch=batch, seq=seq, n_heads=d_sb // HEAD_DIM,
                      col_q=0, col_k=d_sb, col_v=2 * d_sb, nh=4, tq=256, tk=128)
        c0 = 4 * d_sb
        yb = _ch_attn(proj, q_gain3, k_gain3, rel_base, layer,
                      batch=batch, seq=seq, n_heads=d_ch // HEAD_DIM,
                      col_q=c0, col_k=c0 + d_ch, col_v=c0 + 2 * d_ch)
        x2 = _out_proj(x2, ya, yb, proj, 3 * d_sb, c0 + 3 * d_ch, w_out, layer, tm=512, tn=d_model)
    return x2.reshape(batch, seq, d_model)
```

```python
import functools

import jax
import jax.numpy as jnp
import numpy as np
from jax import lax
from jax.experimental import pallas as pl
from jax.experimental.pallas import tpu as pltpu

HEAD_DIM = 128
CHUNK = 64
LEFT_CHUNKS = 8
REL_CLIP = 256
NORM_EPS = 1e-6
NEG_BIG = -1e30
LOG2_E = 1.4426950408889634

V7X_LANES = 128
V7X_VMEM_BYTES = 64 * 1024 * 1024

F32 = jnp.float32
BF16 = jnp.bfloat16


_VMEM_CAP = V7X_VMEM_BYTES - 8 * 1024 * 1024
_VMEM_TEMPS = 24 * 1024 * 1024


def _vmem_limit(block_bytes):
    return int(min(2 * block_bytes + _VMEM_TEMPS, _VMEM_CAP))


_NORM_SLICES = 8

def _in_proj_kernel(x_ref, g_ref, w_ref, cs_ref, o_ref, h_ref):
    rows = x_ref.shape[0] // _NORM_SLICES

    @pl.when(pl.program_id(1) == 0)
    def _():
        w = w_ref[...].astype(BF16)
        for r in range(_NORM_SLICES):
            sl = pl.ds(r * rows, rows)
            x = x_ref[sl, :]
            ms = jnp.mean(x * x, axis=-1, keepdims=True)
            h = (x * lax.rsqrt(ms + NORM_EPS) * g_ref[...]).astype(BF16)
            h_ref[sl, :] = h
            o_ref[sl, :] = (jnp.dot(h, w, preferred_element_type=F32) * cs_ref[...]).astype(o_ref.dtype)

    @pl.when(pl.program_id(1) != 0)
    def _():
        w = w_ref[...].astype(BF16)
        o_ref[...] = (jnp.dot(h_ref[...], w, preferred_element_type=F32) * cs_ref[...]).astype(o_ref.dtype)


def _in_proj(x2, g, w_in, col_scale, layer, *, tm, tn):
    m, d = x2.shape
    n = w_in.shape[2]
    blk = tm * d * 4 + d * tn * 4 + tm * tn * 2 + tm * d * 2
    return pl.pallas_call(
        _in_proj_kernel,
        grid=(m // tm, n // tn),
        in_specs=[
            pl.BlockSpec((tm, d), lambda i, j: (i, 0)),
            pl.BlockSpec((None, 1, d), lambda i, j: (layer, 0, 0)),
            pl.BlockSpec((None, d, tn), lambda i, j: (layer, 0, j)),
            pl.BlockSpec((1, tn), lambda i, j: (0, j)),
        ],
        out_specs=pl.BlockSpec((tm, tn), lambda i, j: (i, j)),
        out_shape=jax.ShapeDtypeStruct((m, n), BF16),
        scratch_shapes=[pltpu.VMEM((tm, d), BF16)],
        compiler_params=pltpu.CompilerParams(
            dimension_semantics=("parallel", "arbitrary"), vmem_limit_bytes=_vmem_limit(blk)),
        name="in_proj",
    )(x2, g, w_in, col_scale)


def _out_proj_kernel(x_ref, ma_ref, mb_ref, w_ref, o_ref, wb_ref):
    @pl.when(pl.program_id(1) == 0)
    def _():
        wb_ref[...] = w_ref[...].astype(BF16)

    da = ma_ref.shape[1]
    acc = jnp.dot(ma_ref[...], wb_ref[:da, :], preferred_element_type=F32)
    acc = acc + jnp.dot(mb_ref[...], wb_ref[da:, :], preferred_element_type=F32)
    o_ref[...] = x_ref[...] + acc


def _out_proj(x2, ma, mb, w_out, layer, *, tm, tn):
    m, d = x2.shape
    da, db = ma.shape[1], mb.shape[1]
    blk = tm * tn * 4 * 2 + tm * (da + db) * 2 + (da + db) * tn * 4 + (da + db) * tn * 2
    return pl.pallas_call(
        _out_proj_kernel,
        grid=(d // tn, m // tm),
        in_specs=[
            pl.BlockSpec((tm, tn), lambda j, i: (i, j)),
            pl.BlockSpec((tm, da), lambda j, i: (i, 0)),
            pl.BlockSpec((tm, db), lambda j, i: (i, 0)),
            pl.BlockSpec((None, da + db, tn), lambda j, i: (layer, 0, j), pipeline_mode=pl.Buffered(1)),
        ],
        out_specs=pl.BlockSpec((tm, tn), lambda j, i: (i, j)),
        out_shape=jax.ShapeDtypeStruct((m, d), F32),
        scratch_shapes=[pltpu.VMEM((da + db, tn), BF16)],
        compiler_params=pltpu.CompilerParams(
            dimension_semantics=("parallel", "arbitrary"), vmem_limit_bytes=_vmem_limit(blk)),
        name="out_proj",
    )(x2, ma, mb, w_out)


def _silu(g):
    return g * (1.0 / (1.0 + jnp.exp(-g)))


def _sb_sets(seq, tq):
    rows = [(i * tq, (i - jj) * tq, i) for i in range(seq // tq) for jj in range(1, i + 1)]
    return np.asarray(rows, np.int32).T.copy()


def _sb_attn_kernel(tbl_ref, q_ref, k_ref, v_ref, g_ref, o_ref,
                    acc_ref, car_ref, z_ref, w_ref, bm_ref, *, nh, tq, tk):
    seq = q_ref.shape[0]
    d = HEAD_DIM
    nq = seq // tq
    nsets = tbl_ref.shape[1]

    row = lax.broadcasted_iota(jnp.int32, (tk, 2 * tk), 0)
    col = lax.broadcasted_iota(jnp.int32, (tk, 2 * tk), 1)
    uj = jnp.where((col >= tk) | (row > col), 1.0, 0.0).astype(BF16)
    t_idx = lax.broadcasted_iota(jnp.int32, (tk, tk), 0)
    s_idx = lax.broadcasted_iota(jnp.int32, (tk, tk), 1)
    bm_ref[...] = jnp.where(s_idx < t_idx, 0.0, NEG_BIG)
    w_ref[...] = jnp.zeros_like(w_ref)

    heads = [pl.ds(h * d, d) for h in range(nh)]

    def split(z):
        sp = jnp.maximum(z, 0.0) + jnp.log2(1.0 + jnp.exp2(-jnp.abs(z)))
        return z - sp, sp.astype(BF16)

    def qk_stage(t0, k0, slot):
        for h, hs in enumerate(heads):
            z_ref[slot, h] = lax.dot_general(q_ref[pl.ds(t0, tq), hs], k_ref[pl.ds(k0, tq), hs],
                                             (((1,), (1,)), ((), ())), preferred_element_type=F32)

    def pv_stage(k0, qi, slot, init):
        for h, hs in enumerate(heads):
            y = jnp.dot(w_ref[slot, h], v_ref[pl.ds(k0, tq), hs], preferred_element_type=F32)
            if init:
                acc_ref[qi, h] = y
            else:
                acc_ref[qi, h] += y

    def mid_diag(qi, zslot, wslot):
        parts = []
        for h in range(nh):
            zl = z_ref[zslot, h, :, :tk]
            zl = jnp.concatenate([zl[:tk] + bm_ref[...], zl[tk:]], axis=0)
            zr = z_ref[zslot, h, tk:, tk:] + bm_ref[...]
            parts.append(split(zl) + split(zr))
        for h in range(nh):
            zs_l, cat_l, zs_r, cat_r = parts[h]
            cs_r = jnp.dot(cat_r, uj, preferred_element_type=F32)
            cs_l = jnp.dot(cat_l, uj, preferred_element_type=F32)
            tot_r = cs_r[:, tk:]
            w_ref[wslot, h, tk:, tk:] = jnp.exp2(zs_r - cs_r[:, :tk]).astype(BF16)
            w_ref[wslot, h, :tk, :tk] = jnp.exp2(zs_l[:tk] - cs_l[:tk, :tk]).astype(BF16)
            w_ref[wslot, h, tk:, :tk] = jnp.exp2(zs_l[tk:] - cs_l[tk:, :tk] - tot_r).astype(BF16)
            car_ref[qi, h, :tk] = cs_l[:tk, tk:]
            car_ref[qi, h, tk:] = tot_r + cs_l[tk:, tk:]

    def mid_full(qi, zslot, wslot):
        parts = []
        for h in range(nh):
            for b in range(2):
                parts.append(split(z_ref[zslot, h, :, pl.ds((1 - b) * tk, tk)]))
        for h in range(nh):
            car = car_ref[qi, h]
            for b in range(2):
                zs, cat = parts[2 * h + b]
                cs = jnp.dot(cat, uj, preferred_element_type=F32)
                w_ref[wslot, h, :, pl.ds((1 - b) * tk, tk)] = jnp.exp2(zs - cs[:, :tk] - car).astype(BF16)
                car = car + cs[:, tk:]
            car_ref[qi, h] = car

    def pipeline(count, qk, pv, mid):
        qk(0, 0)

        def two_sets(m, _):
            n = 2 * m
            qk(n + 1, 1)
            pv(n - 1, 1)
            mid(n, 0, 0)
            qk(jnp.minimum(n + 2, count - 1), 0)
            pv(n, 0)
            mid(n + 1, 1, 1)
            return 0

        lax.fori_loop(0, count // 2, two_sets, 0)
        pv(count - 1, 1)

    def row0(i):
        return pl.multiple_of(i * tq, tq)

    def spare(n, qi):
        return jnp.where(n < 0, nq, qi)

    pipeline(nq,
             lambda i, slot: qk_stage(row0(i), row0(i), slot),
             lambda i, slot: pv_stage(row0(jnp.maximum(i, 0)), spare(i, i), slot, True),
             mid_diag)

    def key0(n):
        return pl.multiple_of(tbl_ref[1, jnp.maximum(n, 0)], tq)

    pipeline(nsets,
             lambda n, slot: qk_stage(pl.multiple_of(tbl_ref[0, n], tq), key0(n), slot),
             lambda n, slot: pv_stage(key0(n), spare(n, tbl_ref[2, jnp.maximum(n, 0)]), slot, False),
             lambda n, zslot, wslot: mid_full(tbl_ref[2, n], zslot, wslot))

    def finish(i, _):
        t0 = row0(i)
        for h, hs in enumerate(heads):
            g = g_ref[pl.ds(t0, tq), hs].astype(F32)
            o_ref[pl.ds(t0, tq), hs] = (acc_ref[i, h] * _silu(g)).astype(o_ref.dtype)
        return 0

    lax.fori_loop(0, nq, finish, 0)


def _sb_attn(proj, *, batch, seq, n_heads, col_q, col_k, col_v, col_g, nh, tq, tk):
    wblk = nh * HEAD_DIM
    nq = seq // tq
    tbl = _sb_sets(seq, tq)
    assert n_heads % nh == 0 and seq % tq == 0 and tq == 2 * tk and tk == V7X_LANES
    assert nq % 2 == 0 and tbl.shape[1] % 2 == 0

    def spec(col):
        base = col // wblk
        return pl.BlockSpec((seq, wblk), lambda b, hg, tbl_ref: (b, base + hg))

    blk = 5 * seq * wblk * 2 + 2 * nq * nh * tq * HEAD_DIM * 4 + nh * tq * tq * (2 * 4 + 2 * 2)
    kern = functools.partial(_sb_attn_kernel, nh=nh, tq=tq, tk=tk)
    return pl.pallas_call(
        kern,
        grid_spec=pltpu.PrefetchScalarGridSpec(
            num_scalar_prefetch=1,
            grid=(batch, n_heads // nh),
            in_specs=[spec(col_q), spec(col_k), spec(col_v), spec(col_g)],
            out_specs=pl.BlockSpec((seq, wblk), lambda b, hg, tbl_ref: (b, hg)),
            scratch_shapes=[pltpu.VMEM((nq + 1, nh, tq, HEAD_DIM), F32), pltpu.VMEM((nq, nh, tq, HEAD_DIM), F32),
                            pltpu.VMEM((2, nh, tq, tq), F32), pltpu.VMEM((2, nh, tq, tq), BF16),
                            pltpu.VMEM((tk, tk), F32)]),
        out_shape=jax.ShapeDtypeStruct((batch * seq, n_heads * HEAD_DIM), BF16),
        compiler_params=pltpu.CompilerParams(
            dimension_semantics=("parallel", "parallel"), vmem_limit_bytes=_vmem_limit(blk)),
        name="sb_attn",
    )(jnp.asarray(tbl), proj, proj, proj, proj)


_CH_TQ = 2 * CHUNK
_CH_WIN = (LEFT_CHUNKS + 2) * CHUNK
_CH_PAD = LEFT_CHUNKS * CHUNK
_CH_BASE = _CH_WIN + _CH_TQ


def _rel_base_index():
    u = np.arange(_CH_BASE)
    u = np.where(u >= _CH_WIN, u - _CH_BASE, u)
    return (np.clip(_CH_PAD - u, -(CHUNK - 1), REL_CLIP) + (CHUNK - 1)).astype(np.int32)


def _ch_attn_kernel(q_ref, k_ref, v_ref, g_ref, qg_ref, kg_ref, base_ref, o_ref,
                    kn_ref, bias_ref, *, scale):
    seq = q_ref.shape[0]
    tq, win = _CH_TQ, _CH_WIN
    nclip = _CH_PAD // tq
    nblk = seq // tq

    def rms(x, gain):
        ms = jnp.mean(x * x, axis=-1, keepdims=True)
        return x * lax.rsqrt(ms + NORM_EPS) * gain

    kn_ref[...] = rms(k_ref[...].astype(F32), kg_ref[...]).astype(BF16)
    q_gain = qg_ref[...] * (scale * LOG2_E)

    @pl.when(pl.program_id(1) == 0)
    def _():
        base = jnp.broadcast_to(base_ref[...] * LOG2_E, (tq, _CH_BASE))
        rolled = pltpu.roll(base, 0, 1, stride=1, stride_axis=0)[:, :win]
        qc = lax.broadcasted_iota(jnp.int32, (tq, win), 0) // CHUNK
        kc = lax.broadcasted_iota(jnp.int32, (tq, win), 1) // CHUNK
        bias_ref[...] = jnp.where((kc >= qc) & (kc <= qc + LEFT_CHUNKS), rolled, NEG_BIG)

    def blocks(t0s, k0s, widths, offs):
        def scores(n):
            qn = rms(q_ref[pl.ds(t0s[n], tq), :].astype(F32), q_gain).astype(BF16)
            kw = kn_ref[pl.ds(k0s[n], widths[n]), :]
            s = lax.dot_general(qn, kw, (((1,), (1,)), ((), ())), preferred_element_type=F32)
            return s + bias_ref[:, offs[n]:offs[n] + widths[n]]

        def softmax(s):
            p = jnp.exp2(s - jnp.max(s, axis=-1, keepdims=True))
            return p.astype(BF16), 1.0 / jnp.sum(p, axis=-1, keepdims=True)

        def output(n, p, inv_l):
            y = jnp.dot(p, v_ref[pl.ds(k0s[n], widths[n]), :], preferred_element_type=F32) * inv_l
            g = g_ref[pl.ds(t0s[n], tq), :].astype(F32)
            o_ref[pl.ds(t0s[n], tq), :] = (y * _silu(g)).astype(o_ref.dtype)

        ss = [scores(n) for n in range(len(t0s))]
        ps = [softmax(s) for s in ss]
        for n, (p, inv_l) in enumerate(ps):
            output(n, p, inv_l)

    for idx in (range(nclip), range(nclip, nblk)):
        blocks([i * tq for i in idx],
               [max(i - nclip, 0) * tq for i in idx],
               [min(i + 1, nclip + 1) * tq for i in idx],
               [max(nclip - i, 0) * tq for i in idx])


def _ch_attn(proj, q_gain, k_gain, rel_base, layer, *, batch, seq, n_heads, col_q, col_k, col_v, col_g):
    d = HEAD_DIM

    def spec(col):
        base = col // d
        return pl.BlockSpec((seq, d), lambda h, b: (b, base + h))

    assert _CH_PAD % _CH_TQ == 0 and seq % _CH_TQ == 0
    kern = functools.partial(_ch_attn_kernel, scale=HEAD_DIM ** -0.5)
    return pl.pallas_call(
        kern,
        grid=(n_heads, batch),
        in_specs=[spec(col_q), spec(col_k), spec(col_v), spec(col_g),
                  pl.BlockSpec((None, 1, d), lambda h, b: (layer, 0, 0)),
                  pl.BlockSpec((None, 1, d), lambda h, b: (layer, 0, 0)),
                  pl.BlockSpec((None, None, 1, _CH_BASE), lambda h, b: (layer, h, 0, 0))],
        out_specs=pl.BlockSpec((seq, d), lambda h, b: (b, h)),
        out_shape=jax.ShapeDtypeStruct((batch * seq, n_heads * d), BF16),
        scratch_shapes=[pltpu.VMEM((seq, d), BF16), pltpu.VMEM((_CH_TQ, _CH_WIN), F32)],
        compiler_params=pltpu.CompilerParams(
            dimension_semantics=("parallel", "arbitrary"), vmem_limit_bytes=_VMEM_CAP),
        name="ch_attn",
    )(proj, proj, proj, proj, q_gain, k_gain, rel_base)


def kernel(x, norm_g, w_in, q_norm_g, k_norm_g, rel_bias, w_out):
    batch, seq, d_model = x.shape
    depth = w_in.shape[0]
    d_mix = w_out.shape[1]
    d_sb = d_mix // 2
    d_ch = d_mix - d_sb
    assert w_in.shape[2] == 4 * d_sb + 4 * d_ch and rel_bias.shape[2] == REL_CLIP + CHUNK
    assert seq % _CH_TQ == 0

    rel_base = rel_bias[:, :, _rel_base_index()][:, :, None, :]
    norm_g3 = norm_g[:, None, :]
    col_scale = np.ones((1, w_in.shape[2]), np.float32)
    col_scale[:, :d_sb] = HEAD_DIM ** -0.5 * LOG2_E
    q_gain3 = q_norm_g[:, None, :]
    k_gain3 = k_norm_g[:, None, :]

    x2 = x.reshape(batch * seq, d_model)
    for layer in range(depth):
        proj = _in_proj(x2, norm_g3, w_in, jnp.asarray(col_scale), layer, tm=1024, tn=1024)
        ma = _sb_attn(proj, batch=batch, seq=seq, n_heads=d_sb // HEAD_DIM,
                      col_q=0, col_k=d_sb, col_v=2 * d_sb, col_g=3 * d_sb, nh=4, tq=256, tk=128)
        c0 = 4 * d_sb
        mb = _ch_attn(proj, q_gain3, k_gain3, rel_base, layer,
                      batch=batch, seq=seq, n_heads=d_ch // HEAD_DIM,
                      col_q=c0, col_k=c0 + d_ch, col_v=c0 + 2 * d_ch, col_g=c0 + 3 * d_ch)
        x2 = _out_proj(x2, ma, mb, w_out, layer, tm=512, tn=d_model)
    return x2.reshape(batch, seq, d_model)
```

```python
import functools

import jax
import jax.numpy as jnp
import numpy as np
from jax import lax
from jax.experimental import pallas as pl
from jax.experimental.pallas import tpu as pltpu

HEAD_DIM = 128
CHUNK = 64
LEFT_CHUNKS = 8
REL_CLIP = 256
NORM_EPS = 1e-6
NEG_BIG = -1e30
LOG2_E = 1.4426950408889634

V7X_LANES = 128
V7X_VMEM_BYTES = 64 * 1024 * 1024

F32 = jnp.float32
BF16 = jnp.bfloat16


_VMEM_CAP = V7X_VMEM_BYTES - 8 * 1024 * 1024
_VMEM_TEMPS = 24 * 1024 * 1024


def _vmem_limit(block_bytes):
    return int(min(2 * block_bytes + _VMEM_TEMPS, _VMEM_CAP))


_NORM_SLICES = 8

def _in_proj_kernel(x_ref, g_ref, w_ref, cs_ref, o_ref, h_ref):
    rows = x_ref.shape[0] // _NORM_SLICES

    @pl.when(pl.program_id(1) == 0)
    def _():
        w = w_ref[...].astype(BF16)
        for r in range(_NORM_SLICES):
            sl = pl.ds(r * rows, rows)
            x = x_ref[sl, :]
            ms = jnp.mean(x * x, axis=-1, keepdims=True)
            h = (x * lax.rsqrt(ms + NORM_EPS) * g_ref[...]).astype(BF16)
            h_ref[sl, :] = h
            o_ref[sl, :] = (jnp.dot(h, w, preferred_element_type=F32) * cs_ref[...]).astype(o_ref.dtype)

    @pl.when(pl.program_id(1) != 0)
    def _():
        w = w_ref[...].astype(BF16)
        o_ref[...] = (jnp.dot(h_ref[...], w, preferred_element_type=F32) * cs_ref[...]).astype(o_ref.dtype)


def _in_proj(x2, g, w_in, col_scale, layer, *, tm, tn):
    m, d = x2.shape
    n = w_in.shape[2]
    blk = tm * d * 4 + d * tn * 4 + tm * tn * 2 + tm * d * 2
    return pl.pallas_call(
        _in_proj_kernel,
        grid=(m // tm, n // tn),
        in_specs=[
            pl.BlockSpec((tm, d), lambda i, j: (i, 0)),
            pl.BlockSpec((None, 1, d), lambda i, j: (layer, 0, 0)),
            pl.BlockSpec((None, d, tn), lambda i, j: (layer, 0, j)),
            pl.BlockSpec((1, tn), lambda i, j: (0, j)),
        ],
        out_specs=pl.BlockSpec((tm, tn), lambda i, j: (i, j)),
        out_shape=jax.ShapeDtypeStruct((m, n), BF16),
        scratch_shapes=[pltpu.VMEM((tm, d), BF16)],
        compiler_params=pltpu.CompilerParams(
            dimension_semantics=("parallel", "arbitrary"), vmem_limit_bytes=_vmem_limit(blk)),
        name="in_proj",
    )(x2, g, w_in, col_scale)


def _out_proj_kernel(x_ref, ma_ref, mb_ref, w_ref, o_ref, wb_ref):
    @pl.when(pl.program_id(1) == 0)
    def _():
        wb_ref[...] = w_ref[...].astype(BF16)

    da = ma_ref.shape[1]
    acc = jnp.dot(ma_ref[...], wb_ref[:da, :], preferred_element_type=F32)
    acc = acc + jnp.dot(mb_ref[...], wb_ref[da:, :], preferred_element_type=F32)
    o_ref[...] = x_ref[...] + acc


def _out_proj(x2, ma, mb, w_out, layer, *, tm, tn):
    m, d = x2.shape
    da, db = ma.shape[1], mb.shape[1]
    blk = tm * tn * 4 * 2 + tm * (da + db) * 2 + (da + db) * tn * 4 + (da + db) * tn * 2
    return pl.pallas_call(
        _out_proj_kernel,
        grid=(d // tn, m // tm),
        in_specs=[
            pl.BlockSpec((tm, tn), lambda j, i: (i, j)),
            pl.BlockSpec((tm, da), lambda j, i: (i, 0)),
            pl.BlockSpec((tm, db), lambda j, i: (i, 0)),
            pl.BlockSpec((None, da + db, tn), lambda j, i: (layer, 0, j), pipeline_mode=pl.Buffered(1)),
        ],
        out_specs=pl.BlockSpec((tm, tn), lambda j, i: (i, j)),
        out_shape=jax.ShapeDtypeStruct((m, d), F32),
        scratch_shapes=[pltpu.VMEM((da + db, tn), BF16)],
        compiler_params=pltpu.CompilerParams(
            dimension_semantics=("parallel", "arbitrary"), vmem_limit_bytes=_vmem_limit(blk)),
        name="out_proj",
    )(x2, ma, mb, w_out)


def _silu(g):
    return g * (1.0 / (1.0 + jnp.exp(-g)))


def _sb_sets(seq, tq):
    rows = [(i * tq, (i - jj) * tq, i) for i in range(seq // tq) for jj in range(1, i + 1)]
    return np.asarray(rows, np.int32).T.copy()


def _sb_attn_kernel(tbl_ref, q_ref, k_ref, v_ref, g_ref, o_ref,
                    acc_ref, car_ref, z_ref, w_ref, bm_ref, *, nh, tq, tk):
    seq = q_ref.shape[0]
    d = HEAD_DIM
    nq = seq // tq
    nsets = tbl_ref.shape[1]

    row = lax.broadcasted_iota(jnp.int32, (tk, 2 * tk), 0)
    col = lax.broadcasted_iota(jnp.int32, (tk, 2 * tk), 1)
    uj = jnp.where((col >= tk) | (row > col), 1.0, 0.0).astype(BF16)
    t_idx = lax.broadcasted_iota(jnp.int32, (tk, tk), 0)
    s_idx = lax.broadcasted_iota(jnp.int32, (tk, tk), 1)
    bm_ref[...] = jnp.where(s_idx < t_idx, 0.0, NEG_BIG)
    w_ref[...] = jnp.zeros_like(w_ref)

    heads = [pl.ds(h * d, d) for h in range(nh)]

    def split(z):
        sp = jnp.maximum(z, 0.0) + jnp.log2(1.0 + jnp.exp2(-jnp.abs(z)))
        return z - sp, sp.astype(BF16)

    def qk_stage(t0, k0, slot):
        for h, hs in enumerate(heads):
            z_ref[slot, h] = lax.dot_general(q_ref[pl.ds(t0, tq), hs], k_ref[pl.ds(k0, tq), hs],
                                             (((1,), (1,)), ((), ())), preferred_element_type=F32)

    def pv_stage(k0, qi, slot, init):
        for h, hs in enumerate(heads):
            y = jnp.dot(w_ref[slot, h], v_ref[pl.ds(k0, tq), hs], preferred_element_type=F32)
            if init:
                acc_ref[qi, h] = y
            else:
                acc_ref[qi, h] += y

    def mid_diag(qi, zslot, wslot):
        parts = []
        for h in range(nh):
            zl = z_ref[zslot, h, :, :tk]
            zl = jnp.concatenate([zl[:tk] + bm_ref[...], zl[tk:]], axis=0)
            zr = z_ref[zslot, h, tk:, tk:] + bm_ref[...]
            parts.append(split(zl) + split(zr))
        for h in range(nh):
            zs_l, cat_l, zs_r, cat_r = parts[h]
            cs_r = jnp.dot(cat_r, uj, preferred_element_type=F32)
            cs_l = jnp.dot(cat_l, uj, preferred_element_type=F32)
            tot_r = cs_r[:, tk:]
            w_ref[wslot, h, tk:, tk:] = jnp.exp2(zs_r - cs_r[:, :tk]).astype(BF16)
            w_ref[wslot, h, :tk, :tk] = jnp.exp2(zs_l[:tk] - cs_l[:tk, :tk]).astype(BF16)
            w_ref[wslot, h, tk:, :tk] = jnp.exp2(zs_l[tk:] - cs_l[tk:, :tk] - tot_r).astype(BF16)
            car_ref[qi, h, :tk] = cs_l[:tk, tk:]
            car_ref[qi, h, tk:] = tot_r + cs_l[tk:, tk:]

    def mid_full(qi, zslot, wslot):
        parts = []
        for h in range(nh):
            for b in range(2):
                parts.append(split(z_ref[zslot, h, :, pl.ds((1 - b) * tk, tk)]))
        for h in range(nh):
            car = car_ref[qi, h]
            for b in range(2):
                zs, cat = parts[2 * h + b]
                cs = jnp.dot(cat, uj, preferred_element_type=F32)
                w_ref[wslot, h, :, pl.ds((1 - b) * tk, tk)] = jnp.exp2(zs - cs[:, :tk] - car).astype(BF16)
                car = car + cs[:, tk:]
            car_ref[qi, h] = car

    def pipeline(count, qk, pv, mid):
        qk(0, 0)

        def two_sets(m, _):
            n = 2 * m
            qk(n + 1, 1)
            pv(n - 1, 1)
            mid(n, 0, 0)
            qk(jnp.minimum(n + 2, count - 1), 0)
            pv(n, 0)
            mid(n + 1, 1, 1)
            return 0

        lax.fori_loop(0, count // 2, two_sets, 0)
        pv(count - 1, 1)

    def row0(i):
        return pl.multiple_of(i * tq, tq)

    def spare(n, qi):
        return jnp.where(n < 0, nq, qi)

    pipeline(nq,
             lambda i, slot: qk_stage(row0(i), row0(i), slot),
             lambda i, slot: pv_stage(row0(jnp.maximum(i, 0)), spare(i, i), slot, True),
             mid_diag)

    def key0(n):
        return pl.multiple_of(tbl_ref[1, jnp.maximum(n, 0)], tq)

    pipeline(nsets,
             lambda n, slot: qk_stage(pl.multiple_of(tbl_ref[0, n], tq), key0(n), slot),
             lambda n, slot: pv_stage(key0(n), spare(n, tbl_ref[2, jnp.maximum(n, 0)]), slot, False),
             lambda n, zslot, wslot: mid_full(tbl_ref[2, n], zslot, wslot))

    def finish(i, _):
        t0 = row0(i)
        for h, hs in enumerate(heads):
            g = g_ref[pl.ds(t0, tq), hs].astype(F32)
            o_ref[pl.ds(t0, tq), hs] = (acc_ref[i, h] * _silu(g)).astype(o_ref.dtype)
        return 0

    lax.fori_loop(0, nq, finish, 0)


def _sb_attn(proj, *, batch, seq, n_heads, col_q, col_k, col_v, col_g, nh, tq, tk):
    wblk = nh * HEAD_DIM
    nq = seq // tq
    tbl = _sb_sets(seq, tq)
    assert n_heads % nh == 0 and seq % tq == 0 and tq == 2 * tk and tk == V7X_LANES
    assert nq % 2 == 0 and tbl.shape[1] % 2 == 0

    def spec(col):
        base = col // wblk
        return pl.BlockSpec((seq, wblk), lambda b, hg, tbl_ref: (b, base + hg))

    blk = 5 * seq * wblk * 2 + 2 * nq * nh * tq * HEAD_DIM * 4 + nh * tq * tq * (2 * 4 + 2 * 2)
    kern = functools.partial(_sb_attn_kernel, nh=nh, tq=tq, tk=tk)
    return pl.pallas_call(
        kern,
        grid_spec=pltpu.PrefetchScalarGridSpec(
            num_scalar_prefetch=1,
            grid=(batch, n_heads // nh),
            in_specs=[spec(col_q), spec(col_k), spec(col_v), spec(col_g)],
            out_specs=pl.BlockSpec((seq, wblk), lambda b, hg, tbl_ref: (b, hg)),
            scratch_shapes=[pltpu.VMEM((nq + 1, nh, tq, HEAD_DIM), F32), pltpu.VMEM((nq, nh, tq, HEAD_DIM), F32),
                            pltpu.VMEM((2, nh, tq, tq), F32), pltpu.VMEM((2, nh, tq, tq), BF16),
                            pltpu.VMEM((tk, tk), F32)]),
        out_shape=jax.ShapeDtypeStruct((batch * seq, n_heads * HEAD_DIM), BF16),
        compiler_params=pltpu.CompilerParams(
            dimension_semantics=("parallel", "parallel"), vmem_limit_bytes=_vmem_limit(blk)),
        name="sb_attn",
    )(jnp.asarray(tbl), proj, proj, proj, proj)


_CH_TQ = 2 * CHUNK
_CH_WIN = (LEFT_CHUNKS + 2) * CHUNK
_CH_PAD = LEFT_CHUNKS * CHUNK
_CH_BASE = _CH_WIN + _CH_TQ
_CH_AHEAD, _CH_BEHIND = 2, 2
_CH_SLOTS = _CH_AHEAD + _CH_BEHIND + 2


def _rel_base_index():
    u = np.arange(_CH_BASE)
    u = np.where(u >= _CH_WIN, u - _CH_BASE, u)
    return (np.clip(_CH_PAD - u, -(CHUNK - 1), REL_CLIP) + (CHUNK - 1)).astype(np.int32)


def _ch_attn_kernel(q_ref, k_ref, v_ref, g_ref, qg_ref, kg_ref, base_ref, o_ref,
                    kn_ref, bias_ref, s_ref, p_ref, il_ref, *, scale):
    seq = q_ref.shape[0]
    tq, win = _CH_TQ, _CH_WIN
    nclip = _CH_PAD // tq
    nblk = seq // tq

    def rms(x, gain):
        ms = jnp.mean(x * x, axis=-1, keepdims=True)
        return x * lax.rsqrt(ms + NORM_EPS) * gain

    kn_ref[...] = rms(k_ref[...].astype(F32), kg_ref[...]).astype(BF16)
    q_gain = qg_ref[...] * (scale * LOG2_E)

    @pl.when(pl.program_id(1) == 0)
    def _():
        base = jnp.broadcast_to(base_ref[...] * LOG2_E, (tq, _CH_BASE))
        rolled = pltpu.roll(base, 0, 1, stride=1, stride_axis=0)[:, :win]
        qc = lax.broadcasted_iota(jnp.int32, (tq, win), 0) // CHUNK
        kc = lax.broadcasted_iota(jnp.int32, (tq, win), 1) // CHUNK
        bias_ref[...] = jnp.where((kc >= qc) & (kc <= qc + LEFT_CHUNKS), rolled, NEG_BIG)

    t0s = [i * tq for i in range(nblk)]
    k0s = [max(i - nclip, 0) * tq for i in range(nblk)]
    wds = [min(i + 1, nclip + 1) * tq for i in range(nblk)]
    offs = [max(nclip - i, 0) * tq for i in range(nblk)]

    def scores(n):
        qn = rms(q_ref[pl.ds(t0s[n], tq), :].astype(F32), q_gain).astype(BF16)
        kw = kn_ref[pl.ds(k0s[n], wds[n]), :]
        s_ref[n % _CH_SLOTS, :, :wds[n]] = lax.dot_general(
            qn, kw, (((1,), (1,)), ((), ())), preferred_element_type=F32)

    def softmax(n):
        s = s_ref[n % _CH_SLOTS, :, :wds[n]] + bias_ref[:, offs[n]:offs[n] + wds[n]]
        p = jnp.exp2(s - jnp.max(s, axis=-1, keepdims=True))
        p_ref[n % _CH_SLOTS, :, :wds[n]] = p.astype(BF16)
        il_ref[n % _CH_SLOTS] = jnp.broadcast_to(1.0 / jnp.sum(p, axis=-1, keepdims=True), (tq, HEAD_DIM))

    def output(n):
        y = jnp.dot(p_ref[n % _CH_SLOTS, :, :wds[n]], v_ref[pl.ds(k0s[n], wds[n]), :],
                    preferred_element_type=F32) * il_ref[n % _CH_SLOTS]
        g = g_ref[pl.ds(t0s[n], tq), :].astype(F32)
        o_ref[pl.ds(t0s[n], tq), :] = (y * _silu(g)).astype(o_ref.dtype)

    for n in range(-_CH_AHEAD, nblk + _CH_BEHIND):
        if 0 <= n + _CH_AHEAD < nblk:
            scores(n + _CH_AHEAD)
        if 0 <= n - _CH_BEHIND < nblk:
            output(n - _CH_BEHIND)
        if 0 <= n < nblk:
            softmax(n)


def _ch_attn(proj, q_gain, k_gain, rel_base, layer, *, batch, seq, n_heads, col_q, col_k, col_v, col_g):
    d = HEAD_DIM

    def spec(col):
        base = col // d
        return pl.BlockSpec((seq, d), lambda h, b: (b, base + h))

    assert _CH_PAD % _CH_TQ == 0 and seq % _CH_TQ == 0
    kern = functools.partial(_ch_attn_kernel, scale=HEAD_DIM ** -0.5)
    return pl.pallas_call(
        kern,
        grid=(n_heads, batch),
        in_specs=[spec(col_q), spec(col_k), spec(col_v), spec(col_g),
                  pl.BlockSpec((None, 1, d), lambda h, b: (layer, 0, 0)),
                  pl.BlockSpec((None, 1, d), lambda h, b: (layer, 0, 0)),
                  pl.BlockSpec((None, None, 1, _CH_BASE), lambda h, b: (layer, h, 0, 0))],
        out_specs=pl.BlockSpec((seq, d), lambda h, b: (b, h)),
        out_shape=jax.ShapeDtypeStruct((batch * seq, n_heads * d), BF16),
        scratch_shapes=[pltpu.VMEM((seq, d), BF16), pltpu.VMEM((_CH_TQ, _CH_WIN), F32),
                        pltpu.VMEM((_CH_SLOTS, _CH_TQ, _CH_WIN), F32), pltpu.VMEM((_CH_SLOTS, _CH_TQ, _CH_WIN), BF16),
                        pltpu.VMEM((_CH_SLOTS, _CH_TQ, HEAD_DIM), F32)],
        compiler_params=pltpu.CompilerParams(
            dimension_semantics=("parallel", "arbitrary"), vmem_limit_bytes=_VMEM_CAP),
        name="ch_attn",
    )(proj, proj, proj, proj, q_gain, k_gain, rel_base)


def kernel(x, norm_g, w_in, q_norm_g, k_norm_g, rel_bias, w_out):
    batch, seq, d_model = x.shape
    depth = w_in.shape[0]
    d_mix = w_out.shape[1]
    d_sb = d_mix // 2
    d_ch = d_mix - d_sb
    assert w_in.shape[2] == 4 * d_sb + 4 * d_ch and rel_bias.shape[2] == REL_CLIP + CHUNK
    assert seq % _CH_TQ == 0

    rel_base = rel_bias[:, :, _rel_base_index()][:, :, None, :]
    norm_g3 = norm_g[:, None, :]
    col_scale = np.ones((1, w_in.shape[2]), np.float32)
    col_scale[:, :d_sb] = HEAD_DIM ** -0.5 * LOG2_E
    q_gain3 = q_norm_g[:, None, :]
    k_gain3 = k_norm_g[:, None, :]

    x2 = x.reshape(batch * seq, d_model)
    for layer in range(depth):
        proj = _in_proj(x2, norm_g3, w_in, jnp.asarray(col_scale), layer, tm=1024, tn=1024)
        ma = _sb_attn(proj, batch=batch, seq=seq, n_heads=d_sb // HEAD_DIM,
                      col_q=0, col_k=d_sb, col_v=2 * d_sb, col_g=3 * d_sb, nh=4, tq=256, tk=128)
        c0 = 4 * d_sb
        mb = _ch_attn(proj, q_gain3, k_gain3, rel_base, layer,
                      batch=batch, seq=seq, n_heads=d_ch // HEAD_DIM,
                      col_q=c0, col_k=c0 + d_ch, col_v=c0 + 2 * d_ch, col_g=c0 + 3 * d_ch)
        x2 = _out_proj(x2, ma, mb, w_out, layer, tm=512, tn=d_model)
    return x2.reshape(batch, seq, d_model)
```

```python
import functools

import jax
import jax.numpy as jnp
import numpy as np
from jax import lax
from jax.experimental import pallas as pl
from jax.experimental.pallas import tpu as pltpu

HEAD_DIM = 128
CHUNK = 64
LEFT_CHUNKS = 8
REL_CLIP = 256
NORM_EPS = 1e-6
NEG_BIG = -1e30
LOG2_E = 1.4426950408889634

V7X_LANES = 128
V7X_VMEM_BYTES = 64 * 1024 * 1024

F32 = jnp.float32
BF16 = jnp.bfloat16


_VMEM_CAP = V7X_VMEM_BYTES - 8 * 1024 * 1024
_VMEM_TEMPS = 24 * 1024 * 1024


def _vmem_limit(block_bytes):
    return int(min(2 * block_bytes + _VMEM_TEMPS, _VMEM_CAP))


_NORM_SLICES = 8

def _in_proj_kernel(x_ref, g_ref, w_ref, cs_ref, o_ref, h_ref):
    rows = x_ref.shape[0] // _NORM_SLICES

    @pl.when(pl.program_id(1) == 0)
    def _():
        w = w_ref[...].astype(BF16)
        for r in range(_NORM_SLICES):
            sl = pl.ds(r * rows, rows)
            x = x_ref[sl, :]
            ms = jnp.mean(x * x, axis=-1, keepdims=True)
            h = (x * lax.rsqrt(ms + NORM_EPS) * g_ref[...]).astype(BF16)
            h_ref[sl, :] = h
            o_ref[sl, :] = (jnp.dot(h, w, preferred_element_type=F32) * cs_ref[...]).astype(o_ref.dtype)

    @pl.when(pl.program_id(1) != 0)
    def _():
        w = w_ref[...].astype(BF16)
        o_ref[...] = (jnp.dot(h_ref[...], w, preferred_element_type=F32) * cs_ref[...]).astype(o_ref.dtype)


def _in_proj(x2, g, w_in, col_scale, layer, *, tm, tn):
    m, d = x2.shape
    n = w_in.shape[2]
    blk = tm * d * 4 + d * tn * 4 + tm * tn * 2 + tm * d * 2
    return pl.pallas_call(
        _in_proj_kernel,
        grid=(m // tm, n // tn),
        in_specs=[
            pl.BlockSpec((tm, d), lambda i, j: (i, 0)),
            pl.BlockSpec((None, 1, d), lambda i, j: (layer, 0, 0)),
            pl.BlockSpec((None, d, tn), lambda i, j: (layer, 0, j)),
            pl.BlockSpec((1, tn), lambda i, j: (0, j)),
        ],
        out_specs=pl.BlockSpec((tm, tn), lambda i, j: (i, j)),
        out_shape=jax.ShapeDtypeStruct((m, n), BF16),
        scratch_shapes=[pltpu.VMEM((tm, d), BF16)],
        compiler_params=pltpu.CompilerParams(
            dimension_semantics=("parallel", "arbitrary"), vmem_limit_bytes=_vmem_limit(blk)),
        name="in_proj",
    )(x2, g, w_in, col_scale)


def _out_proj_kernel(x_ref, ma_ref, mb_ref, w_ref, o_ref, wb_ref):
    @pl.when(pl.program_id(1) == 0)
    def _():
        wb_ref[...] = w_ref[...].astype(BF16)

    da = ma_ref.shape[1]
    acc = jnp.dot(ma_ref[...], wb_ref[:da, :], preferred_element_type=F32)
    acc = acc + jnp.dot(mb_ref[...], wb_ref[da:, :], preferred_element_type=F32)
    o_ref[...] = x_ref[...] + acc


def _out_proj(x2, ma, mb, w_out, layer, *, tm, tn):
    m, d = x2.shape
    da, db = ma.shape[1], mb.shape[1]
    blk = tm * tn * 4 * 2 + tm * (da + db) * 2 + (da + db) * tn * 4 + (da + db) * tn * 2
    return pl.pallas_call(
        _out_proj_kernel,
        grid=(d // tn, m // tm),
        in_specs=[
            pl.BlockSpec((tm, tn), lambda j, i: (i, j)),
            pl.BlockSpec((tm, da), lambda j, i: (i, 0)),
            pl.BlockSpec((tm, db), lambda j, i: (i, 0)),
            pl.BlockSpec((None, da + db, tn), lambda j, i: (layer, 0, j), pipeline_mode=pl.Buffered(1)),
        ],
        out_specs=pl.BlockSpec((tm, tn), lambda j, i: (i, j)),
        out_shape=jax.ShapeDtypeStruct((m, d), F32),
        scratch_shapes=[pltpu.VMEM((da + db, tn), BF16)],
        compiler_params=pltpu.CompilerParams(
            dimension_semantics=("parallel", "arbitrary"), vmem_limit_bytes=_vmem_limit(blk)),
        name="out_proj",
    )(x2, ma, mb, w_out)


def _silu(g):
    return g * (1.0 / (1.0 + jnp.exp(-g)))


def _sb_sets(seq, tq):
    rows = [(i * tq, (i - jj) * tq, i) for i in range(seq // tq) for jj in range(1, i + 1)]
    return np.asarray(rows, np.int32).T.copy()


def _sb_attn_kernel(tbl_ref, q_ref, k_ref, v_ref, g_ref, o_ref,
                    acc_ref, car_ref, z_ref, w_ref, bm_ref, *, nh, tq, tk):
    seq = q_ref.shape[0]
    d = HEAD_DIM
    nq = seq // tq
    nsets = tbl_ref.shape[1]

    row = lax.broadcasted_iota(jnp.int32, (tk, 2 * tk), 0)
    col = lax.broadcasted_iota(jnp.int32, (tk, 2 * tk), 1)
    uj = jnp.where((col >= tk) | (row > col), 1.0, 0.0).astype(BF16)
    t_idx = lax.broadcasted_iota(jnp.int32, (tk, tk), 0)
    s_idx = lax.broadcasted_iota(jnp.int32, (tk, tk), 1)
    bm_ref[...] = jnp.where(s_idx < t_idx, 0.0, NEG_BIG)
    w_ref[...] = jnp.zeros_like(w_ref)

    heads = [pl.ds(h * d, d) for h in range(nh)]

    def split(z):
        sp = jnp.maximum(z, 0.0) + jnp.log2(1.0 + jnp.exp2(-jnp.abs(z)))
        return z - sp, sp.astype(BF16)

    def qk_stage(t0, k0, slot):
        for h, hs in enumerate(heads):
            z_ref[slot, h] = lax.dot_general(q_ref[pl.ds(t0, tq), hs], k_ref[pl.ds(k0, tq), hs],
                                             (((1,), (1,)), ((), ())), preferred_element_type=F32)

    def pv_stage(k0, qi, slot, init):
        for h, hs in enumerate(heads):
            y = jnp.dot(w_ref[slot, h], v_ref[pl.ds(k0, tq), hs], preferred_element_type=F32)
            if init:
                acc_ref[qi, h] = y
            else:
                acc_ref[qi, h] += y

    def mid_diag(qi, zslot, wslot):
        parts = []
        for h in range(nh):
            zl = z_ref[zslot, h, :, :tk]
            zl = jnp.concatenate([zl[:tk] + bm_ref[...], zl[tk:]], axis=0)
            zr = z_ref[zslot, h, tk:, tk:] + bm_ref[...]
            parts.append(split(zl) + split(zr))
        for h in range(nh):
            zs_l, cat_l, zs_r, cat_r = parts[h]
            cs_r = jnp.dot(cat_r, uj, preferred_element_type=F32)
            cs_l = jnp.dot(cat_l, uj, preferred_element_type=F32)
            tot_r = cs_r[:, tk:]
            w_ref[wslot, h, tk:, tk:] = jnp.exp2(zs_r - cs_r[:, :tk]).astype(BF16)
            w_ref[wslot, h, :tk, :tk] = jnp.exp2(zs_l[:tk] - cs_l[:tk, :tk]).astype(BF16)
            w_ref[wslot, h, tk:, :tk] = jnp.exp2(zs_l[tk:] - cs_l[tk:, :tk] - tot_r).astype(BF16)
            car_ref[qi, h, :tk] = cs_l[:tk, tk:]
            car_ref[qi, h, tk:] = tot_r + cs_l[tk:, tk:]

    def mid_full(qi, zslot, wslot):
        parts = []
        for h in range(nh):
            for b in range(2):
                parts.append(split(z_ref[zslot, h, :, pl.ds((1 - b) * tk, tk)]))
        for h in range(nh):
            car = car_ref[qi, h]
            for b in range(2):
                zs, cat = parts[2 * h + b]
                cs = jnp.dot(cat, uj, preferred_element_type=F32)
                w_ref[wslot, h, :, pl.ds((1 - b) * tk, tk)] = jnp.exp2(zs - cs[:, :tk] - car).astype(BF16)
                car = car + cs[:, tk:]
            car_ref[qi, h] = car

    def pipeline(count, qk, pv, mid):
        qk(0, 0)

        def two_sets(m, _):
            n = 2 * m
            qk(n + 1, 1)
            pv(n - 1, 1)
            mid(n, 0, 0)
            qk(jnp.minimum(n + 2, count - 1), 0)
            pv(n, 0)
            mid(n + 1, 1, 1)
            return 0

        lax.fori_loop(0, count // 2, two_sets, 0)
        pv(count - 1, 1)

    def row0(i):
        return pl.multiple_of(i * tq, tq)

    def spare(n, qi):
        return jnp.where(n < 0, nq, qi)

    pipeline(nq,
             lambda i, slot: qk_stage(row0(i), row0(i), slot),
             lambda i, slot: pv_stage(row0(jnp.maximum(i, 0)), spare(i, i), slot, True),
             mid_diag)

    def key0(n):
        return pl.multiple_of(tbl_ref[1, jnp.maximum(n, 0)], tq)

    pipeline(nsets,
             lambda n, slot: qk_stage(pl.multiple_of(tbl_ref[0, n], tq), key0(n), slot),
             lambda n, slot: pv_stage(key0(n), spare(n, tbl_ref[2, jnp.maximum(n, 0)]), slot, False),
             lambda n, zslot, wslot: mid_full(tbl_ref[2, n], zslot, wslot))

    def finish(i, _):
        t0 = row0(i)
        for h, hs in enumerate(heads):
            g = g_ref[pl.ds(t0, tq), hs].astype(F32)
            o_ref[pl.ds(t0, tq), hs] = (acc_ref[i, h] * _silu(g)).astype(o_ref.dtype)
        return 0

    lax.fori_loop(0, nq, finish, 0)


def _sb_attn(proj, *, batch, seq, n_heads, col_q, col_k, col_v, col_g, nh, tq, tk):
    wblk = nh * HEAD_DIM
    nq = seq // tq
    tbl = _sb_sets(seq, tq)
    assert n_heads % nh == 0 and seq % tq == 0 and tq == 2 * tk and tk == V7X_LANES
    assert nq % 2 == 0 and tbl.shape[1] % 2 == 0

    def spec(col):
        base = col // wblk
        return pl.BlockSpec((seq, wblk), lambda b, hg, tbl_ref: (b, base + hg))

    blk = 5 * seq * wblk * 2 + 2 * nq * nh * tq * HEAD_DIM * 4 + nh * tq * tq * (2 * 4 + 2 * 2)
    kern = functools.partial(_sb_attn_kernel, nh=nh, tq=tq, tk=tk)
    return pl.pallas_call(
        kern,
        grid_spec=pltpu.PrefetchScalarGridSpec(
            num_scalar_prefetch=1,
            grid=(batch, n_heads // nh),
            in_specs=[spec(col_q), spec(col_k), spec(col_v), spec(col_g)],
            out_specs=pl.BlockSpec((seq, wblk), lambda b, hg, tbl_ref: (b, hg)),
            scratch_shapes=[pltpu.VMEM((nq + 1, nh, tq, HEAD_DIM), F32), pltpu.VMEM((nq, nh, tq, HEAD_DIM), F32),
                            pltpu.VMEM((2, nh, tq, tq), F32), pltpu.VMEM((2, nh, tq, tq), BF16),
                            pltpu.VMEM((tk, tk), F32)]),
        out_shape=jax.ShapeDtypeStruct((batch * seq, n_heads * HEAD_DIM), BF16),
        compiler_params=pltpu.CompilerParams(
            dimension_semantics=("parallel", "parallel"), vmem_limit_bytes=_vmem_limit(blk)),
        name="sb_attn",
    )(jnp.asarray(tbl), proj, proj, proj, proj)


_CH_TQ = 2 * CHUNK
_CH_WIN = (LEFT_CHUNKS + 2) * CHUNK
_CH_PAD = LEFT_CHUNKS * CHUNK
_CH_BASE = _CH_WIN + _CH_TQ
_CH_AHEAD, _CH_BEHIND = 1, 1
_CH_SLOTS = _CH_AHEAD + _CH_BEHIND + 2


def _rel_base_index():
    u = np.arange(_CH_BASE)
    u = np.where(u >= _CH_WIN, u - _CH_BASE, u)
    return (np.clip(_CH_PAD - u, -(CHUNK - 1), REL_CLIP) + (CHUNK - 1)).astype(np.int32)


def _ch_attn_kernel(q_ref, k_ref, v_ref, g_ref, qg_ref, kg_ref, base_ref, o_ref,
                    kn_ref, bias_ref, s_ref, p_ref, il_ref, *, scale, nhh):
    seq = q_ref.shape[0]
    d = HEAD_DIM
    tq, win = _CH_TQ, _CH_WIN
    nclip = _CH_PAD // tq
    nblk = seq // tq
    hcols = [pl.ds(h * d, d) for h in range(nhh)]

    def rms(x, gain):
        ms = jnp.mean(x * x, axis=-1, keepdims=True)
        return x * lax.rsqrt(ms + NORM_EPS) * gain

    for h in range(nhh):
        kn_ref[h] = rms(k_ref[:, hcols[h]].astype(F32), kg_ref[...]).astype(BF16)
    q_gain = qg_ref[...] * (scale * LOG2_E)

    @pl.when(pl.program_id(1) == 0)
    def _():
        qc = lax.broadcasted_iota(jnp.int32, (tq, win), 0) // CHUNK
        kc = lax.broadcasted_iota(jnp.int32, (tq, win), 1) // CHUNK
        for h in range(nhh):
            base = jnp.broadcast_to(base_ref[h] * LOG2_E, (tq, _CH_BASE))
            rolled = pltpu.roll(base, 0, 1, stride=1, stride_axis=0)[:, :win]
            bias_ref[h] = jnp.where((kc >= qc) & (kc <= qc + LEFT_CHUNKS), rolled, NEG_BIG)

    t0s = [i * tq for i in range(nblk)]
    k0s = [max(i - nclip, 0) * tq for i in range(nblk)]
    wds = [min(i + 1, nclip + 1) * tq for i in range(nblk)]
    offs = [max(nclip - i, 0) * tq for i in range(nblk)]

    def scores(h, n):
        qn = rms(q_ref[pl.ds(t0s[n], tq), hcols[h]].astype(F32), q_gain).astype(BF16)
        kw = kn_ref[h, pl.ds(k0s[n], wds[n]), :]
        s_ref[h, n % _CH_SLOTS, :, :wds[n]] = lax.dot_general(
            qn, kw, (((1,), (1,)), ((), ())), preferred_element_type=F32)

    def softmax(h, n):
        s = s_ref[h, n % _CH_SLOTS, :, :wds[n]] + bias_ref[h, :, offs[n]:offs[n] + wds[n]]
        p = jnp.exp2(s - jnp.max(s, axis=-1, keepdims=True))
        p_ref[h, n % _CH_SLOTS, :, :wds[n]] = p.astype(BF16)
        il_ref[h, n % _CH_SLOTS] = jnp.broadcast_to(1.0 / jnp.sum(p, axis=-1, keepdims=True), (tq, d))

    def output(h, n):
        y = jnp.dot(p_ref[h, n % _CH_SLOTS, :, :wds[n]], v_ref[pl.ds(k0s[n], wds[n]), hcols[h]],
                    preferred_element_type=F32) * il_ref[h, n % _CH_SLOTS]
        g = g_ref[pl.ds(t0s[n], tq), hcols[h]].astype(F32)
        o_ref[pl.ds(t0s[n], tq), hcols[h]] = (y * _silu(g)).astype(o_ref.dtype)

    for n in range(-_CH_AHEAD, nblk + _CH_BEHIND):
        for h in range(nhh):
            if 0 <= n + _CH_AHEAD < nblk:
                scores(h, n + _CH_AHEAD)
        for h in range(nhh):
            if 0 <= n - _CH_BEHIND < nblk:
                output(h, n - _CH_BEHIND)
        for h in range(nhh):
            if 0 <= n < nblk:
                softmax(h, n)


def _ch_attn(proj, q_gain, k_gain, rel_base, layer, *, batch, seq, n_heads, col_q, col_k, col_v, col_g, nhh):
    d = HEAD_DIM
    wblk = nhh * d

    def spec(col):
        base = col // wblk
        return pl.BlockSpec((seq, wblk), lambda h, b: (b, base + h))

    assert _CH_PAD % _CH_TQ == 0 and seq % _CH_TQ == 0 and n_heads % nhh == 0
    kern = functools.partial(_ch_attn_kernel, scale=HEAD_DIM ** -0.5, nhh=nhh)
    return pl.pallas_call(
        kern,
        grid=(n_heads // nhh, batch),
        in_specs=[spec(col_q), spec(col_k), spec(col_v), spec(col_g),
                  pl.BlockSpec((None, 1, d), lambda h, b: (layer, 0, 0)),
                  pl.BlockSpec((None, 1, d), lambda h, b: (layer, 0, 0)),
                  pl.BlockSpec((None, nhh, 1, _CH_BASE), lambda h, b: (layer, h, 0, 0))],
        out_specs=pl.BlockSpec((seq, wblk), lambda h, b: (b, h)),
        out_shape=jax.ShapeDtypeStruct((batch * seq, n_heads * d), BF16),
        scratch_shapes=[pltpu.VMEM((nhh, seq, d), BF16), pltpu.VMEM((nhh, _CH_TQ, _CH_WIN), F32),
                        pltpu.VMEM((nhh, _CH_SLOTS, _CH_TQ, _CH_WIN), F32),
                        pltpu.VMEM((nhh, _CH_SLOTS, _CH_TQ, _CH_WIN), BF16),
                        pltpu.VMEM((nhh, _CH_SLOTS, _CH_TQ, HEAD_DIM), F32)],
        compiler_params=pltpu.CompilerParams(
            dimension_semantics=("parallel", "arbitrary"), vmem_limit_bytes=_VMEM_CAP),
        name="ch_attn",
    )(proj, proj, proj, proj, q_gain, k_gain, rel_base)


def kernel(x, norm_g, w_in, q_norm_g, k_norm_g, rel_bias, w_out):
    batch, seq, d_model = x.shape
    depth = w_in.shape[0]
    d_mix = w_out.shape[1]
    d_sb = d_mix // 2
    d_ch = d_mix - d_sb
    assert w_in.shape[2] == 4 * d_sb + 4 * d_ch and rel_bias.shape[2] == REL_CLIP + CHUNK
    assert seq % _CH_TQ == 0

    rel_base = rel_bias[:, :, _rel_base_index()][:, :, None, :]
    norm_g3 = norm_g[:, None, :]
    col_scale = np.ones((1, w_in.shape[2]), np.float32)
    col_scale[:, :d_sb] = HEAD_DIM ** -0.5 * LOG2_E
    q_gain3 = q_norm_g[:, None, :]
    k_gain3 = k_norm_g[:, None, :]

    x2 = x.reshape(batch * seq, d_model)
    for layer in range(depth):
        proj = _in_proj(x2, norm_g3, w_in, jnp.asarray(col_scale), layer, tm=1024, tn=1024)
        ma = _sb_attn(proj, batch=batch, seq=seq, n_heads=d_sb // HEAD_DIM,
                      col_q=0, col_k=d_sb, col_v=2 * d_sb, col_g=3 * d_sb, nh=4, tq=256, tk=128)
        c0 = 4 * d_sb
        mb = _ch_attn(proj, q_gain3, k_gain3, rel_base, layer,
                      batch=batch, seq=seq, n_heads=d_ch // HEAD_DIM,
                      col_q=c0, col_k=c0 + d_ch, col_v=c0 + 2 * d_ch, col_g=c0 + 3 * d_ch, nhh=2)
        x2 = _out_proj(x2, ma, mb, w_out, layer, tm=512, tn=d_model)
    return x2.reshape(batch, seq, d_model)
```

```python
import functools

import jax
import jax.numpy as jnp
import numpy as np
from jax import lax
from jax.experimental import pallas as pl
from jax.experimental.pallas import tpu as pltpu

HEAD_DIM = 128
CHUNK = 64
LEFT_CHUNKS = 8
REL_CLIP = 256
NORM_EPS = 1e-6
NEG_BIG = -1e30
LOG2_E = 1.4426950408889634

V7X_LANES = 128
V7X_VMEM_BYTES = 64 * 1024 * 1024

F32 = jnp.float32
BF16 = jnp.bfloat16


_VMEM_CAP = V7X_VMEM_BYTES - 8 * 1024 * 1024
_VMEM_TEMPS = 24 * 1024 * 1024


def _vmem_limit(block_bytes):
    return int(min(2 * block_bytes + _VMEM_TEMPS, _VMEM_CAP))


_NORM_SLICES = 8

def _in_proj_kernel(x_ref, g_ref, w_ref, cs_ref, o_ref, h_ref):
    rows = x_ref.shape[0] // _NORM_SLICES

    @pl.when(pl.program_id(1) == 0)
    def _():
        w = w_ref[...].astype(BF16)
        for r in range(_NORM_SLICES):
            sl = pl.ds(r * rows, rows)
            x = x_ref[sl, :]
            ms = jnp.mean(x * x, axis=-1, keepdims=True)
            h = (x * lax.rsqrt(ms + NORM_EPS) * g_ref[...]).astype(BF16)
            h_ref[sl, :] = h
            o_ref[sl, :] = (jnp.dot(h, w, preferred_element_type=F32) * cs_ref[...]).astype(o_ref.dtype)

    @pl.when(pl.program_id(1) != 0)
    def _():
        w = w_ref[...].astype(BF16)
        o_ref[...] = (jnp.dot(h_ref[...], w, preferred_element_type=F32) * cs_ref[...]).astype(o_ref.dtype)


def _in_proj(x2, g, w_in, col_scale, layer, *, tm, tn):
    m, d = x2.shape
    n = w_in.shape[2]
    blk = tm * d * 4 + d * tn * 4 + tm * tn * 2 + tm * d * 2
    return pl.pallas_call(
        _in_proj_kernel,
        grid=(m // tm, n // tn),
        in_specs=[
            pl.BlockSpec((tm, d), lambda i, j: (i, 0)),
            pl.BlockSpec((None, 1, d), lambda i, j: (layer, 0, 0)),
            pl.BlockSpec((None, d, tn), lambda i, j: (layer, 0, j)),
            pl.BlockSpec((1, tn), lambda i, j: (0, j)),
        ],
        out_specs=pl.BlockSpec((tm, tn), lambda i, j: (i, j)),
        out_shape=jax.ShapeDtypeStruct((m, n), BF16),
        scratch_shapes=[pltpu.VMEM((tm, d), BF16)],
        compiler_params=pltpu.CompilerParams(
            dimension_semantics=("parallel", "arbitrary"), vmem_limit_bytes=_vmem_limit(blk)),
        name="in_proj",
    )(x2, g, w_in, col_scale)


def _out_proj_kernel(x_ref, ma_ref, mb_ref, w_ref, o_ref, wb_ref):
    @pl.when(pl.program_id(1) == 0)
    def _():
        wb_ref[...] = w_ref[...].astype(BF16)

    da = ma_ref.shape[1]
    acc = jnp.dot(ma_ref[...], wb_ref[:da, :], preferred_element_type=F32)
    acc = acc + jnp.dot(mb_ref[...], wb_ref[da:, :], preferred_element_type=F32)
    o_ref[...] = x_ref[...] + acc


def _out_proj(x2, ma, mb, w_out, layer, *, tm, tn):
    m, d = x2.shape
    da, db = ma.shape[1], mb.shape[1]
    blk = tm * tn * 4 * 2 + tm * (da + db) * 2 + (da + db) * tn * 4 + (da + db) * tn * 2
    return pl.pallas_call(
        _out_proj_kernel,
        grid=(d // tn, m // tm),
        in_specs=[
            pl.BlockSpec((tm, tn), lambda j, i: (i, j)),
            pl.BlockSpec((tm, da), lambda j, i: (i, 0)),
            pl.BlockSpec((tm, db), lambda j, i: (i, 0)),
            pl.BlockSpec((None, da + db, tn), lambda j, i: (layer, 0, j), pipeline_mode=pl.Buffered(1)),
        ],
        out_specs=pl.BlockSpec((tm, tn), lambda j, i: (i, j)),
        out_shape=jax.ShapeDtypeStruct((m, d), F32),
        scratch_shapes=[pltpu.VMEM((da + db, tn), BF16)],
        compiler_params=pltpu.CompilerParams(
            dimension_semantics=("parallel", "arbitrary"), vmem_limit_bytes=_vmem_limit(blk)),
        name="out_proj",
    )(x2, ma, mb, w_out)


def _silu(g):
    return g * (1.0 / (1.0 + jnp.exp(-g)))


def _sb_sets(seq, tq):
    rows = [(i * tq, (i - jj) * tq, i) for i in range(seq // tq) for jj in range(1, i + 1)]
    return np.asarray(rows, np.int32).T.copy()


def _sb_attn_kernel(tbl_ref, q_ref, k_ref, v_ref, g_ref, o_ref,
                    acc_ref, car_ref, z_ref, w_ref, bm_ref, *, nh, tq, tk):
    seq = q_ref.shape[0]
    d = HEAD_DIM
    nq = seq // tq
    nsets = tbl_ref.shape[1]

    row = lax.broadcasted_iota(jnp.int32, (tk, 2 * tk), 0)
    col = lax.broadcasted_iota(jnp.int32, (tk, 2 * tk), 1)
    uj = jnp.where((col >= tk) | (row > col), 1.0, 0.0).astype(BF16)
    t_idx = lax.broadcasted_iota(jnp.int32, (tk, tk), 0)
    s_idx = lax.broadcasted_iota(jnp.int32, (tk, tk), 1)
    bm_ref[...] = jnp.where(s_idx < t_idx, 0.0, NEG_BIG)
    w_ref[...] = jnp.zeros_like(w_ref)

    heads = [pl.ds(h * d, d) for h in range(nh)]

    def split(z):
        sp = jnp.maximum(z, 0.0) + jnp.log2(1.0 + jnp.exp2(-jnp.abs(z)))
        return z - sp, sp.astype(BF16)

    def qk_stage(t0, k0, slot):
        for h, hs in enumerate(heads):
            z_ref[slot, h] = lax.dot_general(q_ref[pl.ds(t0, tq), hs], k_ref[pl.ds(k0, tq), hs],
                                             (((1,), (1,)), ((), ())), preferred_element_type=F32)

    def pv_stage(k0, qi, slot, init):
        for h, hs in enumerate(heads):
            y = jnp.dot(w_ref[slot, h], v_ref[pl.ds(k0, tq), hs], preferred_element_type=F32)
            if init:
                acc_ref[qi, h] = y
            else:
                acc_ref[qi, h] += y

    def mid_diag(qi, zslot, wslot):
        parts = []
        for h in range(nh):
            zl = z_ref[zslot, h, :, :tk]
            zl = jnp.concatenate([zl[:tk] + bm_ref[...], zl[tk:]], axis=0)
            zr = z_ref[zslot, h, tk:, tk:] + bm_ref[...]
            parts.append(split(zl) + split(zr))
        for h in range(nh):
            zs_l, cat_l, zs_r, cat_r = parts[h]
            cs_r = jnp.dot(cat_r, uj, preferred_element_type=F32)
            cs_l = jnp.dot(cat_l, uj, preferred_element_type=F32)
            tot_r = cs_r[:, tk:]
            w_ref[wslot, h, tk:, tk:] = jnp.exp2(zs_r - cs_r[:, :tk]).astype(BF16)
            w_ref[wslot, h, :tk, :tk] = jnp.exp2(zs_l[:tk] - cs_l[:tk, :tk]).astype(BF16)
            w_ref[wslot, h, tk:, :tk] = jnp.exp2(zs_l[tk:] - cs_l[tk:, :tk] - tot_r).astype(BF16)
            car_ref[qi, h, :tk] = cs_l[:tk, tk:]
            car_ref[qi, h, tk:] = tot_r + cs_l[tk:, tk:]

    def mid_full(qi, zslot, wslot):
        parts = []
        for h in range(nh):
            for b in range(2):
                parts.append(split(z_ref[zslot, h, :, pl.ds((1 - b) * tk, tk)]))
        for h in range(nh):
            car = car_ref[qi, h]
            for b in range(2):
                zs, cat = parts[2 * h + b]
                cs = jnp.dot(cat, uj, preferred_element_type=F32)
                w_ref[wslot, h, :, pl.ds((1 - b) * tk, tk)] = jnp.exp2(zs - cs[:, :tk] - car).astype(BF16)
                car = car + cs[:, tk:]
            car_ref[qi, h] = car

    def pipeline(count, qk, pv, mid):
        qk(0, 0)

        def two_sets(m, _):
            n = 2 * m
            qk(n + 1, 1)
            pv(n - 1, 1)
            mid(n, 0, 0)
            qk(jnp.minimum(n + 2, count - 1), 0)
            pv(n, 0)
            mid(n + 1, 1, 1)
            return 0

        lax.fori_loop(0, count // 2, two_sets, 0)
        pv(count - 1, 1)

    def row0(i):
        return pl.multiple_of(i * tq, tq)

    def spare(n, qi):
        return jnp.where(n < 0, nq, qi)

    pipeline(nq,
             lambda i, slot: qk_stage(row0(i), row0(i), slot),
             lambda i, slot: pv_stage(row0(jnp.maximum(i, 0)), spare(i, i), slot, True),
             mid_diag)

    def key0(n):
        return pl.multiple_of(tbl_ref[1, jnp.maximum(n, 0)], tq)

    pipeline(nsets,
             lambda n, slot: qk_stage(pl.multiple_of(tbl_ref[0, n], tq), key0(n), slot),
             lambda n, slot: pv_stage(key0(n), spare(n, tbl_ref[2, jnp.maximum(n, 0)]), slot, False),
             lambda n, zslot, wslot: mid_full(tbl_ref[2, n], zslot, wslot))

    def finish(i, _):
        t0 = row0(i)
        for h, hs in enumerate(heads):
            g = g_ref[pl.ds(t0, tq), hs].astype(F32)
            o_ref[pl.ds(t0, tq), hs] = (acc_ref[i, h] * _silu(g)).astype(o_ref.dtype)
        return 0

    lax.fori_loop(0, nq, finish, 0)


def _sb_attn(proj, *, batch, seq, n_heads, col_q, col_k, col_v, col_g, nh, tq, tk):
    wblk = nh * HEAD_DIM
    nq = seq // tq
    tbl = _sb_sets(seq, tq)
    assert n_heads % nh == 0 and seq % tq == 0 and tq == 2 * tk and tk == V7X_LANES
    assert nq % 2 == 0 and tbl.shape[1] % 2 == 0

    def spec(col):
        base = col // wblk
        return pl.BlockSpec((seq, wblk), lambda b, hg, tbl_ref: (b, base + hg))

    blk = 5 * seq * wblk * 2 + 2 * nq * nh * tq * HEAD_DIM * 4 + nh * tq * tq * (2 * 4 + 2 * 2)
    kern = functools.partial(_sb_attn_kernel, nh=nh, tq=tq, tk=tk)
    return pl.pallas_call(
        kern,
        grid_spec=pltpu.PrefetchScalarGridSpec(
            num_scalar_prefetch=1,
            grid=(batch, n_heads // nh),
            in_specs=[spec(col_q), spec(col_k), spec(col_v), spec(col_g)],
            out_specs=pl.BlockSpec((seq, wblk), lambda b, hg, tbl_ref: (b, hg)),
            scratch_shapes=[pltpu.VMEM((nq + 1, nh, tq, HEAD_DIM), F32), pltpu.VMEM((nq, nh, tq, HEAD_DIM), F32),
                            pltpu.VMEM((2, nh, tq, tq), F32), pltpu.VMEM((2, nh, tq, tq), BF16),
                            pltpu.VMEM((tk, tk), F32)]),
        out_shape=jax.ShapeDtypeStruct((batch * seq, n_heads * HEAD_DIM), BF16),
        compiler_params=pltpu.CompilerParams(
            dimension_semantics=("parallel", "parallel"), vmem_limit_bytes=_vmem_limit(blk)),
        name="sb_attn",
    )(jnp.asarray(tbl), proj, proj, proj, proj)


_CH_TQ = 2 * CHUNK
_CH_WIN = (LEFT_CHUNKS + 2) * CHUNK
_CH_PAD = LEFT_CHUNKS * CHUNK
_CH_BASE = _CH_WIN + _CH_TQ
_CH_AHEAD, _CH_BEHIND = 1, 1
_CH_SLOTS = _CH_AHEAD + _CH_BEHIND + 2


def _rel_base_index():
    u = np.arange(_CH_BASE)
    u = np.where(u >= _CH_WIN, u - _CH_BASE, u)
    return (np.clip(_CH_PAD - u, -(CHUNK - 1), REL_CLIP) + (CHUNK - 1)).astype(np.int32)


def _ch_attn_kernel(q_ref, k_ref, v_ref, g_ref, qg_ref, kg_ref, base_ref, o_ref,
                    kn_ref, bias_ref, s_ref, p_ref, il_ref, *, scale, nhh):
    seq = q_ref.shape[0]
    d = HEAD_DIM
    tq, win = _CH_TQ, _CH_WIN
    nclip = _CH_PAD // tq
    nblk = seq // tq
    hcols = [pl.ds(h * d, d) for h in range(nhh)]

    def rms(x, gain):
        ms = jnp.mean(x * x, axis=-1, keepdims=True)
        return x * lax.rsqrt(ms + NORM_EPS) * gain

    for h in range(nhh):
        kn_ref[h] = rms(k_ref[:, hcols[h]].astype(F32), kg_ref[...]).astype(BF16)
    q_gain = qg_ref[...] * (scale * LOG2_E)

    @pl.when(pl.program_id(1) == 0)
    def _():
        qc = lax.broadcasted_iota(jnp.int32, (tq, win), 0) // CHUNK
        kc = lax.broadcasted_iota(jnp.int32, (tq, win), 1) // CHUNK
        for h in range(nhh):
            base = jnp.broadcast_to(base_ref[h] * LOG2_E, (tq, _CH_BASE))
            rolled = pltpu.roll(base, 0, 1, stride=1, stride_axis=0)[:, :win]
            bias_ref[h] = jnp.where((kc >= qc) & (kc <= qc + LEFT_CHUNKS), rolled, NEG_BIG)

    t0s = [i * tq for i in range(nblk)]
    k0s = [max(i - nclip, 0) * tq for i in range(nblk)]
    wds = [min(i + 1, nclip + 1) * tq for i in range(nblk)]
    offs = [max(nclip - i, 0) * tq for i in range(nblk)]

    def scores(h, n):
        qn = rms(q_ref[pl.ds(t0s[n], tq), hcols[h]].astype(F32), q_gain).astype(BF16)
        kw = kn_ref[h, pl.ds(k0s[n], wds[n]), :]
        s_ref[h, n % _CH_SLOTS, :, :wds[n]] = lax.dot_general(
            qn, kw, (((1,), (1,)), ((), ())), preferred_element_type=F32)

    def softmax(h, n):
        s = s_ref[h, n % _CH_SLOTS, :, :wds[n]] + bias_ref[h, :, offs[n]:offs[n] + wds[n]]
        p = jnp.exp2(s - jnp.max(s, axis=-1, keepdims=True))
        p_ref[h, n % _CH_SLOTS, :, :wds[n]] = p.astype(BF16)
        il_ref[h, n % _CH_SLOTS] = jnp.broadcast_to(1.0 / jnp.sum(p, axis=-1, keepdims=True), (tq, d))

    def output(h, n):
        y = jnp.dot(p_ref[h, n % _CH_SLOTS, :, :wds[n]], v_ref[pl.ds(k0s[n], wds[n]), hcols[h]],
                    preferred_element_type=F32) * il_ref[h, n % _CH_SLOTS]
        g = g_ref[pl.ds(t0s[n], tq), hcols[h]].astype(F32)
        o_ref[pl.ds(t0s[n], tq), hcols[h]] = (y * _silu(g)).astype(o_ref.dtype)

    for n in range(-_CH_AHEAD, nblk + _CH_BEHIND):
        for h in range(nhh):
            if 0 <= n + _CH_AHEAD < nblk:
                scores(h, n + _CH_AHEAD)
        for h in range(nhh):
            if 0 <= n - _CH_BEHIND < nblk:
                output(h, n - _CH_BEHIND)
        for h in range(nhh):
            if 0 <= n < nblk:
                softmax(h, n)


def _ch_attn(proj, q_gain, k_gain, rel_base, layer, *, batch, seq, n_heads, col_q, col_k, col_v, col_g, nhh):
    d = HEAD_DIM
    wblk = nhh * d

    def spec(col):
        base = col // wblk
        return pl.BlockSpec((seq, wblk), lambda h, b: (b, base + h))

    assert _CH_PAD % _CH_TQ == 0 and seq % _CH_TQ == 0 and n_heads % nhh == 0
    kern = functools.partial(_ch_attn_kernel, scale=HEAD_DIM ** -0.5, nhh=nhh)
    return pl.pallas_call(
        kern,
        grid=(n_heads // nhh, batch),
        in_specs=[spec(col_q), spec(col_k), spec(col_v), spec(col_g),
                  pl.BlockSpec((None, 1, d), lambda h, b: (layer, 0, 0)),
                  pl.BlockSpec((None, 1, d), lambda h, b: (layer, 0, 0)),
                  pl.BlockSpec((None, nhh, 1, _CH_BASE), lambda h, b: (layer, h, 0, 0))],
        out_specs=pl.BlockSpec((seq, wblk), lambda h, b: (b, h)),
        out_shape=jax.ShapeDtypeStruct((batch * seq, n_heads * d), BF16),
        scratch_shapes=[pltpu.VMEM((nhh, seq, d), BF16), pltpu.VMEM((nhh, _CH_TQ, _CH_WIN), F32),
                        pltpu.VMEM((nhh, _CH_SLOTS, _CH_TQ, _CH_WIN), F32),
                        pltpu.VMEM((nhh, _CH_SLOTS, _CH_TQ, _CH_WIN), BF16),
                        pltpu.VMEM((nhh, _CH_SLOTS, _CH_TQ, HEAD_DIM), F32)],
        compiler_params=pltpu.CompilerParams(
            dimension_semantics=("parallel", "arbitrary"), vmem_limit_bytes=_VMEM_CAP),
        name="ch_attn",
    )(proj, proj, proj, proj, q_gain, k_gain, rel_base)


def kernel(x, norm_g, w_in, q_norm_g, k_norm_g, rel_bias, w_out):
    batch, seq, d_model = x.shape
    depth = w_in.shape[0]
    d_mix = w_out.shape[1]
    d_sb = d_mix // 2
    d_ch = d_mix - d_sb
    assert w_in.shape[2] == 4 * d_sb + 4 * d_ch and rel_bias.shape[2] == REL_CLIP + CHUNK
    assert seq % _CH_TQ == 0

    rel_base = rel_bias[:, :, _rel_base_index()][:, :, None, :]
    norm_g3 = norm_g[:, None, :]
    col_scale = np.ones((1, w_in.shape[2]), np.float32)
    col_scale[:, :d_sb] = HEAD_DIM ** -0.5 * LOG2_E
    q_gain3 = q_norm_g[:, None, :]
    k_gain3 = k_norm_g[:, None, :]

    x2 = x.reshape(batch * seq, d_model)
    for layer in range(depth):
        proj = _in_proj(x2, norm_g3, w_in, jnp.asarray(col_scale), layer, tm=1024, tn=1024)
        ma = _sb_attn(proj, batch=batch, seq=seq, n_heads=d_sb // HEAD_DIM,
                      col_q=0, col_k=d_sb, col_v=2 * d_sb, col_g=3 * d_sb, nh=4, tq=256, tk=128)
        c0 = 4 * d_sb
        mb = _ch_attn(proj, q_gain3, k_gain3, rel_base, layer,
                      batch=batch, seq=seq, n_heads=d_ch // HEAD_DIM,
                      col_q=c0, col_k=c0 + d_ch, col_v=c0 + 2 * d_ch, col_g=c0 + 3 * d_ch, nhh=4)
        x2 = _out_proj(x2, ma, mb, w_out, layer, tm=512, tn=d_model)
    return x2.reshape(batch, seq, d_model)
```
